```python
import jax, jax.numpy as jnp
from jax import lax
import numpy as np

D_MODEL = 1024
BATCH = 2
SEQ = 8192
DEPTH = 2
DEC_BATCH = 32
DEC_SEQ = 4
PAST_LEN = 16384
PAGE_SIZE = 128

HEAD_DIM = 64
N_GDN_HEADS = 8
GDN_DK = HEAD_DIM
GDN_DV = HEAD_DIM
N_SWA_HEADS = 8
GDN_QKV = 3 * N_GDN_HEADS * GDN_DK
GDN_WIDTH = N_GDN_HEADS * GDN_DV
SWA_WIDTH = N_SWA_HEADS * HEAD_DIM
MIX_WIDTH = GDN_WIDTH + SWA_WIDTH
PROJ_SIZES = (GDN_QKV, GDN_WIDTH, N_GDN_HEADS, N_GDN_HEADS, SWA_WIDTH, SWA_WIDTH, SWA_WIDTH)
PROJ_WIDTH = sum(PROJ_SIZES)
PROJ_SPLITS = [int(c) for c in np.cumsum(PROJ_SIZES)[:-1]]
CONV_W = 4
GDN_CHUNK = 64
DILATIONS = ((128, 1), (512, 4), (2048, 16))
WINDOW_MAX = 2048
QB = 128
ROPE_THETA = 10000.0
D_FF = 2816
N_EXPERTS = 8
TOP_K = 2
D_FF_EXPERT = 3584
N_DENSE = (DEPTH + 1) // 2
N_MOE = DEPTH // 2
EPS = 1e-6

kernel_name = 'hybrid_gdn_dilated_swa_decoder_step'


def rmsnorm(x, g):
    xf = x.astype(jnp.float32)
    y = xf * lax.rsqrt(jnp.mean(xf * xf, axis=-1, keepdims=True) + EPS)
    return (y * g.astype(jnp.float32)).astype(x.dtype)


def l2norm(x):
    return x * lax.rsqrt(jnp.sum(x * x, axis=-1, keepdims=True) + EPS)


def rotary(x, pos):
    e = x.shape[-1]
    half = e // 2
    inv_freq = jnp.power(jnp.float32(ROPE_THETA), -jnp.arange(half, dtype=jnp.float32) * (2.0 / e))
    ang = pos.astype(jnp.float32)[:, None] * inv_freq[None, :]
    cos = jnp.cos(ang)[None, :, None, :]
    sin = jnp.sin(ang)[None, :, None, :]
    x = x.astype(jnp.float32)
    x1, x2 = x[..., :half], x[..., half:]
    return jnp.concatenate([x1 * cos - x2 * sin, x2 * cos + x1 * sin], axis=-1)


def causal_short_conv(x_pre, prev, w):
    xp = jnp.concatenate([prev.astype(jnp.float32), x_pre.astype(jnp.float32)], axis=1)
    out = lax.conv_general_dilated(xp, w.astype(jnp.float32)[:, None, :], window_strides=(1,), padding='VALID',
                                   dimension_numbers=('NWC', 'WIO', 'NWC'), feature_group_count=xp.shape[-1])
    return jax.nn.silu(out), xp[:, xp.shape[1] - (CONV_W - 1):]


def gated_delta_rule(q, k, v, g, beta, s0):
    bn, t_len, h, dk = q.shape
    dv = v.shape[-1]
    c = min(GDN_CHUNK, t_len)
    tp = -(-t_len // c) * c
    n = tp // c

    def chunks(t):
        t = jnp.pad(t, [(0, 0), (0, tp - t_len)] + [(0, 0)] * (t.ndim - 2))
        t = t.reshape(bn, n, c, *t.shape[2:])
        return jnp.moveaxis(t, 2, 3)

    qc, kc, vc, gc, bc = chunks(q), chunks(k), chunks(v), chunks(g), chunks(beta)
    gcum = jnp.cumsum(gc, axis=-1)
    diff = gcum[..., :, None] - gcum[..., None, :]
    ii = jnp.arange(c)
    lower = ii[:, None] >= ii[None, :]
    strict = ii[:, None] > ii[None, :]
    decay = jnp.exp(jnp.where(lower, diff, -jnp.inf))
    decay_strict = jnp.where(strict, decay, 0.0)
    kk = jnp.einsum('bnhid,bnhjd->bnhij', kc, kc)
    lmat = jnp.eye(c, dtype=jnp.float32) + bc[..., :, None] * kk * decay_strict
    rhs = jnp.concatenate([bc[..., None] * vc, (bc * jnp.exp(gcum))[..., None] * kc], axis=-1)
    sol = lax.linalg.triangular_solve(lmat, rhs, left_side=True, lower=True)
    u, wk = sol[..., :dv], sol[..., dv:]
    aqk = jnp.einsum('bnhid,bnhjd->bnhij', qc, kc) * decay
    q_dec = qc * jnp.exp(gcum)[..., None]
    k_dec = kc * jnp.exp(gcum[..., -1:] - gcum)[..., None]
    g_last = jnp.exp(gcum[..., -1])

    def step(s, xs):
        u_c, wk_c, aqk_c, qd_c, kd_c, gl_c = xs
        w = u_c - jnp.einsum('bhcd,bhde->bhce', wk_c, s)
        o = jnp.einsum('bhcd,bhde->bhce', qd_c, s) + jnp.einsum('bhij,bhje->bhie', aqk_c, w)
        s = gl_c[..., None, None] * s + jnp.einsum('bhcd,bhce->bhde', kd_c, w)
        return s, o

    xs = tuple(jnp.moveaxis(t, 1, 0) for t in (u, wk, aqk, q_dec, k_dec, g_last))
    s_final, o = lax.scan(step, s0, xs)
    o = jnp.moveaxis(jnp.moveaxis(o, 0, 1), 2, 3).reshape(bn, tp, h, dv)[:, :t_len]
    return o, s_final


def gdn_mixer(a_qkv, a_z, a_beta, a_alpha, conv_prev, s0, conv_w, a_log, dt_bias, norm_g):
    bn, t_len, _ = a_qkv.shape
    cq, conv_new = causal_short_conv(a_qkv, conv_prev, conv_w)
    q, k, v = jnp.split(cq, 3, axis=-1)
    q = l2norm(q.reshape(bn, t_len, N_GDN_HEADS, GDN_DK)) * (GDN_DK ** -0.5)
    k = l2norm(k.reshape(bn, t_len, N_GDN_HEADS, GDN_DK))
    v = v.reshape(bn, t_len, N_GDN_HEADS, GDN_DV)
    beta = jax.nn.sigmoid(a_beta.astype(jnp.float32))
    g = -jnp.exp(a_log.astype(jnp.float32)) * jax.nn.softplus(a_alpha.astype(jnp.float32) + dt_bias.astype(jnp.float32))
    o, s_new = gated_delta_rule(q, k, v, g, beta, s0.astype(jnp.float32))
    z = a_z.astype(jnp.float32).reshape(bn, t_len, N_GDN_HEADS, GDN_DV)
    o = o * lax.rsqrt(jnp.mean(o * o, axis=-1, keepdims=True) + EPS) * norm_g.astype(jnp.float32) * jax.nn.silu(z)
    return o.reshape(bn, t_len, GDN_WIDTH), conv_new, s_new


def dilated_prompt(q, k, v, window, dilation):
    bn, s_len, h, e = q.shape
    band = window // dilation
    nb = -(-band // QB)
    unit = dilation * QB
    sp = -(-s_len // unit) * unit
    m_len = sp // dilation
    nblk = m_len // QB

    def to_sub(t):
        t = jnp.pad(t, ((0, 0), (0, sp - s_len), (0, 0), (0, 0)))
        t = jnp.swapaxes(t.reshape(bn, m_len, dilation, h, e), 1, 2)
        return t.reshape(bn, dilation, nblk, QB, h, e)

    def band_rows(t):
        tp = jnp.pad(t, ((0, 0), (0, 0), (nb, 0), (0, 0), (0, 0), (0, 0)))
        return jnp.concatenate([tp[:, :, i:i + nblk] for i in range(nb + 1)], axis=3)

    def from_sub(t):
        rest = t.shape[4:]
        t = jnp.swapaxes(t.reshape(bn, dilation, m_len, *rest), 1, 2)
        return t.reshape(bn, sp, *rest)[:, :s_len]

    qs = to_sub(q)
    kb = band_rows(to_sub(k))
    vb = band_rows(to_sub(v))
    s = jnp.einsum('bdnqhe,bdnkhe->bdnqhk', qs, kb) * (e ** -0.5)
    qi = (jnp.arange(nblk)[:, None] * QB + jnp.arange(QB)[None, :])[:, :, None]
    ki = ((jnp.arange(nblk)[:, None] - nb) * QB + jnp.arange((nb + 1) * QB)[None, :])[:, None, :]
    dist = qi - ki
    valid = (dist >= 0) & (dist <= band) & (ki >= 0)
    s = jnp.where(valid[None, None, :, :, None, :], s, -jnp.inf)
    m = jnp.max(s, axis=-1)
    p = jnp.exp(s - m[..., None])
    den = jnp.sum(p, axis=-1)
    num = jnp.einsum('bdnqhk,bdnkhe->bdnqhe', p, vb)
    return from_sub(num), from_sub(den), from_sub(m)


def dilated_sample(q, k_all, v_all, window, dilation, n_buf):
    t_len, e = q.shape[1], q.shape[-1]
    band = window // dilation
    idx = n_buf + jnp.arange(t_len)[:, None] - dilation * jnp.arange(band + 1)[None, :]
    valid = idx >= 0
    idx = jnp.maximum(idx, 0)
    kg = k_all[:, idx]
    vg = v_all[:, idx]
    s = jnp.einsum('bthe,btkhe->bthk', q, kg) * (e ** -0.5)
    s = jnp.where(valid[None, :, None, :], s, -jnp.inf)
    m = jnp.max(s, axis=-1)
    p = jnp.exp(s - m[..., None])
    den = jnp.sum(p, axis=-1)
    num = jnp.einsum('bthk,btkhe->bthe', p, vg)
    return num, den, m


def combine_by_denominator(parts):
    num = jnp.stack([pt[0] for pt in parts])
    den = jnp.stack([pt[1] for pt in parts])
    m = jnp.stack([pt[2] for pt in parts])
    w = jnp.exp(m - jnp.max(m, axis=0, keepdims=True))
    return jnp.sum(w[..., None] * num, axis=0) / jnp.sum(w * den, axis=0)[..., None]


def token_mixing(x, pos, conv_prev, s0, win_k, win_v, norm_g, w_in_l, conv_w_l, a_log_l, dt_bias_l, gdn_g_l, w_out_l):
    bn, t_len, _ = x.shape
    proj = rmsnorm(x, norm_g) @ w_in_l
    a_qkv, a_z, a_beta, a_alpha, b_q, b_k, b_v = jnp.split(proj, PROJ_SPLITS, axis=-1)
    o_a, conv_new, s_new = gdn_mixer(a_qkv, a_z, a_beta, a_alpha, conv_prev, s0, conv_w_l, a_log_l, dt_bias_l, gdn_g_l)
    q = rotary(b_q.reshape(bn, t_len, N_SWA_HEADS, HEAD_DIM), pos)
    k = rotary(b_k.reshape(bn, t_len, N_SWA_HEADS, HEAD_DIM), pos)
    v = b_v.reshape(bn, t_len, N_SWA_HEADS, HEAD_DIM).astype(jnp.float32)
    if win_k is None:
        parts = [dilated_prompt(q, k, v, w, d) for (w, d) in DILATIONS]
        n_keep = min(WINDOW_MAX, t_len)
        new_k, new_v = k[:, t_len - n_keep:], v[:, t_len - n_keep:]
    else:
        n_buf = win_k.shape[1]
        k_all = jnp.concatenate([win_k.astype(jnp.float32), k], axis=1)
        v_all = jnp.concatenate([win_v.astype(jnp.float32), v], axis=1)
        parts = [dilated_sample(q, k_all, v_all, w, d, n_buf) for (w, d) in DILATIONS]
        new_k, new_v = k_all[:, t_len:], v_all[:, t_len:]
    o_b = combine_by_denominator(parts).reshape(bn, t_len, SWA_WIDTH)
    out = jnp.concatenate([o_a, o_b], axis=-1).astype(x.dtype) @ w_out_l
    return out, conv_new, s_new, new_k, new_v


def swiglu(t, wg, wu, wd):
    return ((jax.nn.silu(t @ wg) * (t @ wu)) @ wd).astype(t.dtype)


def moe_swiglu(x, router_w, wg, wu, wd):
    shp = x.shape
    t = x.reshape(-1, shp[-1])
    logits = (t @ router_w).astype(jnp.float32)
    top_v, top_i = lax.top_k(logits, TOP_K)
    wts = jax.nn.softmax(top_v, axis=-1)
    gates = jnp.sum(jax.nn.one_hot(top_i, N_EXPERTS, dtype=jnp.float32) * wts[..., None], axis=1)
    out = jnp.zeros(t.shape, jnp.float32)
    for e in range(N_EXPERTS):
        h = jax.nn.silu(t @ wg[e]) * (t @ wu[e])
        out = out + gates[:, e:e + 1] * (h @ wd[e])
    return out.astype(x.dtype).reshape(shp)


def setup_inputs(seed: int = 0) -> dict:
    key = jax.random.key(seed)
    ks = jax.random.split(key, 24)
    f32 = jnp.float32

    def nrm(k, shape, scale=1.0):
        return jax.random.normal(k, shape, f32) * scale

    wb = min(WINDOW_MAX, PAST_LEN)
    dt = jnp.exp(jax.random.uniform(ks[9], (DEPTH, N_GDN_HEADS), f32, np.log(1e-3), np.log(1e-1)))
    return {
        'x_prompt': nrm(ks[0], (BATCH, SEQ, D_MODEL)),
        'x_sample': nrm(ks[1], (DEC_BATCH, DEC_SEQ, D_MODEL)),
        'state_conv': nrm(ks[2], (DEPTH, DEC_BATCH, CONV_W - 1, GDN_QKV)),
        'state_delta': nrm(ks[3], (DEPTH, DEC_BATCH, N_GDN_HEADS, GDN_DK, GDN_DV), 0.05),
        'cache_win_k': nrm(ks[4], (DEPTH, DEC_BATCH, wb, N_SWA_HEADS, HEAD_DIM)),
        'cache_win_v': nrm(ks[5], (DEPTH, DEC_BATCH, wb, N_SWA_HEADS, HEAD_DIM)),
        'norm_mix': 1.0 + nrm(ks[6], (DEPTH, D_MODEL), 0.02),
        'w_in': nrm(ks[7], (DEPTH, D_MODEL, PROJ_WIDTH), D_MODEL ** -0.5),
        'conv_w': nrm(ks[8], (DEPTH, CONV_W, GDN_QKV), CONV_W ** -0.5),
        'a_log': jnp.log(jax.random.uniform(ks[10], (DEPTH, N_GDN_HEADS), f32, 1.0, 16.0)),
        'dt_bias': dt + jnp.log(-jnp.expm1(-dt)),
        'gdn_norm': 1.0 + nrm(ks[11], (DEPTH, GDN_DV), 0.02),
        'w_out': nrm(ks[12], (DEPTH, MIX_WIDTH, D_MODEL), MIX_WIDTH ** -0.5),
        'norm_ffn': 1.0 + nrm(ks[13], (DEPTH, D_MODEL), 0.02),
        'ffn_gate': nrm(ks[14], (N_DENSE, D_MODEL, D_FF), D_MODEL ** -0.5),
        'ffn_up': nrm(ks[15], (N_DENSE, D_MODEL, D_FF), D_MODEL ** -0.5),
        'ffn_down': nrm(ks[16], (N_DENSE, D_FF, D_MODEL), D_FF ** -0.5),
        'router': nrm(ks[17], (N_MOE, D_MODEL, N_EXPERTS), D_MODEL ** -0.5),
        'moe_gate': nrm(ks[18], (N_MOE, N_EXPERTS, D_MODEL, D_FF_EXPERT), D_MODEL ** -0.5),
        'moe_up': nrm(ks[19], (N_MOE, N_EXPERTS, D_MODEL, D_FF_EXPERT), D_MODEL ** -0.5),
        'moe_down': nrm(ks[20], (N_MOE, N_EXPERTS, D_FF_EXPERT, D_MODEL), D_FF_EXPERT ** -0.5),
        'final_norm': 1.0 + nrm(ks[21], (D_MODEL,), 0.02),
    }


def reference(x_prompt, x_sample, state_conv, state_delta, cache_win_k, cache_win_v, norm_mix, w_in, conv_w,
              a_log, dt_bias, gdn_norm, w_out, norm_ffn, ffn_gate, ffn_up, ffn_down, router, moe_gate, moe_up,
              moe_down, final_norm):
    bp, s_len, _ = x_prompt.shape
    t_len = x_sample.shape[1]
    pos_p = jnp.arange(s_len)
    pos_s = PAST_LEN + jnp.arange(t_len)
    conv0 = jnp.zeros((bp, CONV_W - 1, GDN_QKV), x_prompt.dtype)
    delta0 = jnp.zeros((bp, N_GDN_HEADS, GDN_DK, GDN_DV), jnp.float32)
    xp, xs = x_prompt, x_sample
    conv_p, delta_p, wk_p, wv_p = [], [], [], []
    conv_s, delta_s, wk_s, wv_s = [], [], [], []
    for l in range(DEPTH):
        lw = (norm_mix[l], w_in[l], conv_w[l], a_log[l], dt_bias[l], gdn_norm[l], w_out[l])
        hp, cp, dp, kp, vp = token_mixing(xp, pos_p, conv0, delta0, None, None, *lw)
        hs, cs, ds, kq, vq = token_mixing(xs, pos_s, state_conv[l], state_delta[l], cache_win_k[l], cache_win_v[l], *lw)
        xp = xp + hp
        xs = xs + hs
        np_ = rmsnorm(xp, norm_ffn[l])
        ns_ = rmsnorm(xs, norm_ffn[l])
        i = l // 2
        if l % 2 == 0:
            xp = xp + swiglu(np_, ffn_gate[i], ffn_up[i], ffn_down[i])
            xs = xs + swiglu(ns_, ffn_gate[i], ffn_up[i], ffn_down[i])
        else:
            xp = xp + moe_swiglu(np_, router[i], moe_gate[i], moe_up[i], moe_down[i])
            xs = xs + moe_swiglu(ns_, router[i], moe_gate[i], moe_up[i], moe_down[i])
        conv_p.append(cp); delta_p.append(dp); wk_p.append(kp); wv_p.append(vp)
        conv_s.append(cs); delta_s.append(ds); wk_s.append(kq); wv_s.append(vq)
    y_prompt = rmsnorm(xp, final_norm)
    y_sample = rmsnorm(xs, final_norm)
    return (y_prompt, y_sample, jnp.stack(conv_p), jnp.stack(delta_p), jnp.stack(wk_p), jnp.stack(wv_p),
            jnp.stack(conv_s), jnp.stack(delta_s), jnp.stack(wk_s), jnp.stack(wv_s))
```

```python
import functools

import numpy as np
import jax
import jax.numpy as jnp
from jax import lax
from jax.experimental import pallas as pl
from jax.experimental.pallas import tpu as pltpu

F32 = jnp.float32
BF16 = jnp.bfloat16
I32 = jnp.int32

EPS = 1e-6
NEG_BIG = -1e30
D_MODEL = 1024
HEAD_DIM = 64
N_HEADS = 8
GROUP_W = N_HEADS * HEAD_DIM
QKV_W = 3 * GROUP_W
CONV_TAPS = 4
CHUNK = 64
HEADS_PER_PACK = 4
PACK_W = HEADS_PER_PACK * HEAD_DIM
DILATIONS = ((128, 1), (512, 4), (2048, 16))
BAND = 128
WINDOW_MAX = 2048
ROPE_THETA = 10000.0
PAST_LEN = 16384
N_EXPERTS = 8
PROJ_PACKED_W = QKV_W + 4 * GROUP_W + 128
LANES = 128
VMEM_LIMIT = 56 * 1024 * 1024


def _dot(a, b):
    return jnp.dot(a, b, preferred_element_type=F32)


def _dot_nt(a, b):
    return lax.dot_general(a, b, (((1,), (1,)), ((), ())), preferred_element_type=F32)


def _mm(a, b, precise=False):
    if precise:
        return jnp.dot(a.astype(F32), b.astype(F32), preferred_element_type=F32, precision=lax.Precision.HIGHEST)
    return jnp.dot(a.astype(BF16), b.astype(BF16), preferred_element_type=F32)


def _mm_nt(a, b, precise=False):
    if precise:
        return lax.dot_general(a.astype(F32), b.astype(F32), (((1,), (1,)), ((), ())), preferred_element_type=F32,
                               precision=lax.Precision.HIGHEST)
    return _dot_nt(a.astype(BF16), b.astype(BF16))


def _split2(x):
    hi = x.astype(BF16)
    lo = (x - hi.astype(F32)).astype(BF16)
    return hi, lo


def _split3(x):
    hi = x.astype(BF16)
    r = x - hi.astype(F32)
    mid = r.astype(BF16)
    lo = (r - mid.astype(F32)).astype(BF16)
    return hi, mid, lo


def _dot_x3(a, b):
    ah, al = _split2(a)
    bh, bl = _split2(b)
    m = a.shape[0]
    r = _dot(jnp.concatenate([ah, al], axis=0), bh)
    return r[:m] + r[m:] + _dot(ah, bl)


def _dot_exact_rhs01(x, e):
    return sum(_dot(p, e) for p in _split3(x))


def _sigmoid(x):
    return 1.0 / (1.0 + jnp.exp(-x))


def _silu(x):
    return x * _sigmoid(x)


def _softplus(x):
    return jnp.maximum(x, 0.0) + jnp.log1p(jnp.exp(-jnp.abs(x)))


def _rmsnorm(x, g):
    ms = jnp.mean(x * x, axis=-1, keepdims=True)
    return x * lax.rsqrt(ms + EPS) * g


def _head_ones(width):
    r = lax.broadcasted_iota(I32, (width, width), 0) // HEAD_DIM
    c = lax.broadcasted_iota(I32, (width, width), 1) // HEAD_DIM
    return jnp.where(r == c, 1.0, 0.0).astype(BF16)


def _head_sum(x, ones_bd):
    hi, lo = _split2(x)
    return _dot(hi, ones_bd) + _dot(lo, ones_bd)


def _rope(x, cos, sin):
    n = x.shape[1]
    reps = n // LANES
    c = jnp.concatenate([cos] * reps, axis=1)
    s = jnp.concatenate([sin] * reps, axis=1)
    lane = lax.broadcasted_iota(I32, x.shape, 1)
    first_half = (lane & (HEAD_DIM - 1)) < HEAD_DIM // 2
    partner = jnp.where(first_half, pltpu.roll(x, n - HEAD_DIM // 2, 1), pltpu.roll(x, HEAD_DIM // 2, 1))
    return x * c + partner * s


def _inproj_kernel(x_ref, g_ref, w_ref, cos_ref, sin_ref, qkv_ref, z_ref, gt_ref, q_ref, k_ref, v_ref, *, precise):
    y = _rmsnorm(x_ref[...], g_ref[...])
    if not precise:
        y = y.astype(BF16)

    def mm(lo, hi):
        return _mm(y, w_ref[:, lo:hi], precise)

    cos = cos_ref[...]
    sin = sin_ref[...]
    o = 0
    qkv_ref[...] = mm(o, o + QKV_W)
    o += QKV_W
    z_ref[...] = mm(o, o + GROUP_W)
    o += GROUP_W
    q_ref[...] = _rope(mm(o, o + GROUP_W), cos, sin)
    o += GROUP_W
    k_ref[...] = _rope(mm(o, o + GROUP_W), cos, sin)
    o += GROUP_W
    v_ref[...] = mm(o, o + GROUP_W)
    o += GROUP_W
    gt_ref[...] = mm(o, o + LANES)


def _inproj_call(x, g, w_packed, cos_tab, sin_tab, tm, precise=False):
    m = x.shape[0]
    n_tab = cos_tab.shape[0] // tm
    row = lambda i: (i, 0)
    const = lambda i: (0, 0)
    tab = lambda i: (i % n_tab, 0)
    outs = [(QKV_W, F32), (GROUP_W, F32), (LANES, F32), (GROUP_W, F32), (GROUP_W, F32), (GROUP_W, F32)]
    return pl.pallas_call(
        functools.partial(_inproj_kernel, precise=precise),
        grid=(m // tm,),
        in_specs=[
            pl.BlockSpec((tm, D_MODEL), row),
            pl.BlockSpec((1, D_MODEL), const),
            pl.BlockSpec((D_MODEL, PROJ_PACKED_W), const),
            pl.BlockSpec((tm, LANES), tab),
            pl.BlockSpec((tm, LANES), tab),
        ],
        out_specs=[pl.BlockSpec((tm, w), row) for w, _ in outs],
        out_shape=[jax.ShapeDtypeStruct((m, w), dt) for w, dt in outs],
        compiler_params=pltpu.CompilerParams(dimension_semantics=("arbitrary",), vmem_limit_bytes=VMEM_LIMIT),
        name="inproj",
    )(x, g, w_packed, cos_tab, sin_tab)


def _pack_masks():
    lane = lax.broadcasted_iota(I32, (1, PACK_W), 1)
    return [lane // HEAD_DIM == h for h in range(HEADS_PER_PACK)]


def _block_diag(x, head_masks):
    return jnp.concatenate([jnp.where(m, x, 0.0) for m in head_masks], axis=0)


def _gdn_chunk_consts():
    i = lax.broadcasted_iota(I32, (CHUNK, PACK_W), 0)
    j = lax.broadcasted_iota(I32, (CHUNK, PACK_W), 1) & (HEAD_DIM - 1)
    r = lax.broadcasted_iota(I32, (CHUNK, CHUNK), 0)
    c = lax.broadcasted_iota(I32, (CHUNK, CHUNK), 1)
    br = lax.broadcasted_iota(I32, (PACK_W, PACK_W), 0) // HEAD_DIM
    bc = lax.broadcasted_iota(I32, (PACK_W, PACK_W), 1) // HEAD_DIM
    return dict(
        head_masks=_pack_masks(),
        lower=i >= j,
        strict=i > j,
        upper_f=jnp.where(i <= j, 1.0, 0.0),
        eye=jnp.where(i == j, 1.0, 0.0),
        tri=jnp.where(r >= c, 1.0, 0.0).astype(BF16),
        block=br == bc,
    )


def _gdn_chunk(qn, kn, v, bw, gw, state, cst, precise):
    hm = cst["head_masks"]
    gcum = sum(_dot(cst["tri"], p) for p in _split3(gw))
    grow = jnp.sum(gw * cst["upper_f"], axis=0, keepdims=True)
    diff = gcum - grow
    decay = jnp.where(cst["lower"], jnp.exp(jnp.where(cst["lower"], diff, 0.0)), 0.0)
    kq = _mm_nt(jnp.concatenate([kn, qn], axis=0), _block_diag(kn, hm), precise)
    kk, qk = kq[:CHUNK], kq[CHUNK:]
    nmat = bw * kk * jnp.where(cst["strict"], decay, 0.0)
    p = -nmat
    y = cst["eye"] + p
    q = _dot_x3(p, _block_diag(p, hm))
    n_sq = CHUNK.bit_length() - 1
    for s in range(1, n_sq):
        q_bd = _block_diag(q, hm)
        if s < n_sq - 1:
            both = _dot_x3(jnp.concatenate([y, q], axis=0), q_bd)
            y = y + both[:CHUNK]
            q = both[CHUNK:]
        else:
            y = y + _dot_x3(y, q_bd)
    egc = jnp.exp(gcum)
    rhs = jnp.concatenate([_block_diag(bw * v, hm), _block_diag(bw * egc * kn, hm)], axis=1)
    sol = _mm(y, rhs, precise)
    u, wk = sol[:, :PACK_W], sol[:, PACK_W:]
    aqk = qk * decay
    qd = qn * egc
    glast = gcum[CHUNK - 1:CHUNK, :]
    kd = kn * jnp.exp(glast - gcum)
    m1 = _mm(jnp.concatenate([wk, qd], axis=0), state, precise)
    w = u - m1[:CHUNK]
    o = m1[CHUNK:] + _mm(aqk, _block_diag(w, hm), precise)
    inc = _mm(jnp.transpose(kd), w, precise)
    new_state = state * jnp.exp(glast) + jnp.where(cst["block"], inc, 0.0)
    return o, new_state


def _gdn_kernel(*refs, nb, rows, carry, n_valid, precise):
    if carry:
        (qkv_ref, prev_ref, z_ref, gt_ref, cw_ref, al_ref, db_ref, gn_ref,
         o_ref, sout_ref, xp_s, q_s, k_s, v_s, b_s, g_s, o_s, st_s) = refs
        s0_ref = None
    else:
        (qkv_ref, prev_ref, z_ref, gt_ref, cw_ref, al_ref, db_ref, gn_ref, s0_ref,
         o_ref, sout_ref, xp_s, q_s, k_s, v_s, b_s, g_s, o_s, st_s) = refs
    step = pl.program_id(0)
    ones_bd = _head_ones(GROUP_W)
    lane = lax.broadcasted_iota(I32, (1, LANES), 1)
    er = lax.broadcasted_iota(I32, (LANES, 2 * GROUP_W), 0)
    ec = lax.broadcasted_iota(I32, (LANES, 2 * GROUP_W), 1)
    expand = jnp.where(((ec < GROUP_W) & (er == ec // HEAD_DIM))
                       | ((ec >= GROUP_W) & (er == N_HEADS + (ec - GROUP_W) // HEAD_DIM)), 1.0, 0.0).astype(BF16)
    cw = cw_ref[...]

    if carry:
        @pl.when(step == 0)
        def _():
            st_s[...] = jnp.zeros_like(st_s)
    else:
        st_s[...] = s0_ref[...]

    for b in range(nb):
        prev = prev_ref[b]
        if carry:
            prev = jnp.where(step == 0, 0.0, prev)
        xp_s[b, 0:8, :] = prev
        xp_s[b, 8:8 + rows, :] = qkv_ref[b]
        acc = cw[CONV_TAPS - 1:CONV_TAPS, :] * xp_s[b, 8:8 + rows, :]
        for t in range(1, CONV_TAPS):
            acc = acc + cw[CONV_TAPS - 1 - t:CONV_TAPS - t, :] * xp_s[b, 8 - t:8 - t + rows, :]
        cq = _silu(acc)
        q = cq[:, :GROUP_W]
        k = cq[:, GROUP_W:2 * GROUP_W]
        v = cq[:, 2 * GROUP_W:]
        qn = q * lax.rsqrt(_head_sum(q * q, ones_bd) + EPS) * (HEAD_DIM ** -0.5)
        kn = k * lax.rsqrt(_head_sum(k * k, ones_bd) + EPS)
        t_raw = gt_ref[b]
        beta = _sigmoid(t_raw)
        gdec = -jnp.exp(al_ref[...]) * _softplus(t_raw + db_ref[...])
        wide = _dot_exact_rhs01(jnp.where(lane < N_HEADS, beta, gdec), expand)
        bw = wide[:, :GROUP_W]
        gw = wide[:, GROUP_W:]
        if n_valid < rows:
            live = lax.broadcasted_iota(I32, (rows, 1), 0) < n_valid
            bw = jnp.where(live, bw, 0.0)
            gw = jnp.where(live, gw, 0.0)
            qn = jnp.where(live, qn, 0.0)
            kn = jnp.where(live, kn, 0.0)
            v = jnp.where(live, v, 0.0)
        q_s[b] = qn
        k_s[b] = kn
        v_s[b] = v
        b_s[b] = bw
        g_s[b] = gw

    cst = _gdn_chunk_consts()

    def chunk_body(c, carry_val):
        r0 = pl.multiple_of(c * CHUNK, CHUNK)
        for b in range(nb):
            for gi in range(N_HEADS // HEADS_PER_PACK):
                ls = slice(gi * PACK_W, (gi + 1) * PACK_W)
                rs = pl.ds(r0, CHUNK)
                o, new_state = _gdn_chunk(q_s[b, rs, ls], k_s[b, rs, ls], v_s[b, rs, ls],
                                          b_s[b, rs, ls], g_s[b, rs, ls], st_s[b, gi], cst, precise)
                o_s[b, rs, ls] = o
                st_s[b, gi] = new_state
        return carry_val

    n_chunks = rows // CHUNK
    if n_chunks == 1:
        chunk_body(0, 0)
    else:
        lax.fori_loop(0, n_chunks, chunk_body, 0)

    for b in range(nb):
        o = o_s[b]
        ms = _head_sum(o * o, ones_bd) * (1.0 / HEAD_DIM)
        o_ref[b] = (o * lax.rsqrt(ms + EPS) * gn_ref[...] * _silu(z_ref[b])).astype(o_ref.dtype)

    if carry:
        @pl.when(step == pl.num_programs(0) - 1)
        def _():
            sout_ref[...] = st_s[...]
    else:
        sout_ref[...] = st_s[...]


def _gdn_scratch(nb, rows):
    act = pltpu.VMEM((nb, rows, GROUP_W), F32)
    return [pltpu.VMEM((nb, rows + 8, QKV_W), F32), act, act, act, act, act, act,
            pltpu.VMEM((nb, N_HEADS // HEADS_PER_PACK, PACK_W, PACK_W), F32)]


def _gdn_prompt_call(qkv, z, gt, conv_w, alog_row, dtb_row, gn_row, batch, seq, rows):
    qkv3 = qkv.reshape(batch, seq, QKV_W)
    z3 = z.reshape(batch, seq, GROUP_W)
    gt3 = gt.reshape(batch, seq, LANES)
    blk = lambda i: (0, i, 0)
    const2 = lambda i: (0, 0)
    n_pack = N_HEADS // HEADS_PER_PACK
    return pl.pallas_call(
        functools.partial(_gdn_kernel, nb=batch, rows=rows, carry=True, n_valid=rows, precise=False),
        grid=(seq // rows,),
        in_specs=[
            pl.BlockSpec((batch, rows, QKV_W), blk),
            pl.BlockSpec((batch, 8, QKV_W), lambda i: (0, jnp.maximum(i * (rows // 8) - 1, 0), 0)),
            pl.BlockSpec((batch, rows, GROUP_W), blk),
            pl.BlockSpec((batch, rows, LANES), blk),
            pl.BlockSpec((CONV_TAPS, QKV_W), const2),
            pl.BlockSpec((1, LANES), const2),
            pl.BlockSpec((1, LANES), const2),
            pl.BlockSpec((1, GROUP_W), const2),
        ],
        out_specs=[
            pl.BlockSpec((batch, rows, GROUP_W), blk),
            pl.BlockSpec((batch, n_pack, PACK_W, PACK_W), lambda i: (0, 0, 0, 0)),
        ],
        out_shape=[
            jax.ShapeDtypeStruct((batch, seq, GROUP_W), BF16),
            jax.ShapeDtypeStruct((batch, n_pack, PACK_W, PACK_W), F32),
        ],
        scratch_shapes=_gdn_scratch(batch, rows),
        compiler_params=pltpu.CompilerParams(dimension_semantics=("arbitrary",), vmem_limit_bytes=VMEM_LIMIT),
        name="gdn_prompt",
    )(qkv3, qkv3, z3, gt3, conv_w, alog_row, dtb_row, gn_row)


def _gdn_sample_call(qkv_pad, prev_pad, z_pad, gt_pad, conv_w, alog_row, dtb_row, gn_row, s0_bd, n_valid, nb):
    n_seq = qkv_pad.shape[0]
    blk = lambda i: (i, 0, 0)
    blk4 = lambda i: (i, 0, 0, 0)
    const2 = lambda i: (0, 0)
    n_pack = N_HEADS // HEADS_PER_PACK
    return pl.pallas_call(
        functools.partial(_gdn_kernel, nb=nb, rows=CHUNK, carry=False, n_valid=n_valid, precise=True),
        grid=(n_seq // nb,),
        in_specs=[
            pl.BlockSpec((nb, CHUNK, QKV_W), blk),
            pl.BlockSpec((nb, 8, QKV_W), blk),
            pl.BlockSpec((nb, CHUNK, GROUP_W), blk),
            pl.BlockSpec((nb, CHUNK, LANES), blk),
            pl.BlockSpec((CONV_TAPS, QKV_W), const2),
            pl.BlockSpec((1, LANES), const2),
            pl.BlockSpec((1, LANES), const2),
            pl.BlockSpec((1, GROUP_W), const2),
            pl.BlockSpec((nb, n_pack, PACK_W, PACK_W), blk4),
        ],
        out_specs=[
            pl.BlockSpec((nb, CHUNK, GROUP_W), blk),
            pl.BlockSpec((nb, n_pack, PACK_W, PACK_W), blk4),
        ],
        out_shape=[
            jax.ShapeDtypeStruct((n_seq, CHUNK, GROUP_W), F32),
            jax.ShapeDtypeStruct((n_seq, n_pack, PACK_W, PACK_W), F32),
        ],
        scratch_shapes=_gdn_scratch(nb, CHUNK),
        compiler_params=pltpu.CompilerParams(dimension_semantics=("arbitrary",), vmem_limit_bytes=VMEM_LIMIT),
        name="gdn_sample",
    )(qkv_pad, prev_pad, z_pad, gt_pad, conv_w, alog_row, dtb_row, gn_row, s0_bd)


def _state_to_block_diag(s):
    n = s.shape[0]
    n_pack = N_HEADS // HEADS_PER_PACK
    s5 = s.reshape(n, n_pack, HEADS_PER_PACK, HEAD_DIM, HEAD_DIM).astype(F32)
    eye = jnp.eye(HEADS_PER_PACK, dtype=F32)
    return jnp.einsum("bghde,hk->bghdke", s5, eye).reshape(n, n_pack, PACK_W, PACK_W)


def _state_from_block_diag(s_bd):
    n = s_bd.shape[0]
    n_pack = N_HEADS // HEADS_PER_PACK
    s6 = s_bd.reshape(n, n_pack, HEADS_PER_PACK, HEAD_DIM, HEADS_PER_PACK, HEAD_DIM)
    diag = jnp.stack([s6[:, :, h, :, h, :] for h in range(HEADS_PER_PACK)], axis=2)
    return diag.reshape(n, N_HEADS, HEAD_DIM, HEAD_DIM)


def _swa_kernel(q_ref, kc_ref, kp_ref, vc_ref, vp_ref, o_ref, l_ref, *, tq):
    first_block = pl.program_id(2) == 0
    n_sub = tq // BAND
    lane = lax.broadcasted_iota(I32, (1, LANES), 1)
    low = lane < HEAD_DIM
    qi = lax.broadcasted_iota(I32, (BAND, 2 * BAND), 0)
    kj = lax.broadcasted_iota(I32, (BAND, 2 * BAND), 1)
    dist = qi + BAND - kj
    band = (dist >= 0) & (dist <= BAND)
    band_first = band & ((kj >= BAND) | jnp.logical_not(first_block))
    lane_l = lax.broadcasted_iota(I32, (BAND, LANES), 1)
    lse_acc = [jnp.zeros((BAND, LANES), F32) for _ in range(n_sub)]
    for p in range(GROUP_W // LANES):
        ls = slice(p * LANES, (p + 1) * LANES)
        q_pair = (q_ref[:, ls] * (HEAD_DIM ** -0.5)).astype(BF16)
        k_all = jnp.concatenate([kp_ref[:, ls], kc_ref[:, ls]], axis=0)
        v_all = jnp.concatenate([vp_ref[:, ls], vc_ref[:, ls]], axis=0)
        o_sub = [jnp.zeros((BAND, LANES), F32) for _ in range(n_sub)]
        for hh in range(2):
            hmask = low if hh == 0 else jnp.logical_not(low)
            k_h = jnp.where(hmask, k_all, 0.0).astype(BF16)
            v_h = jnp.where(hmask, v_all, 0.0).astype(BF16)
            head = 2 * p + hh
            for s in range(n_sub):
                sc = _dot_nt(q_pair[s * BAND:(s + 1) * BAND], k_h[s * BAND:s * BAND + 2 * BAND])
                valid = band_first if s == 0 else band
                sc = jnp.where(valid, sc, NEG_BIG)
                m = jnp.max(sc, axis=1, keepdims=True)
                pm = jnp.exp(sc - m)
                den = jnp.sum(pm, axis=1, keepdims=True)
                num = _dot(pm.astype(BF16), v_h[s * BAND:s * BAND + 2 * BAND])
                o_sub[s] = o_sub[s] + num / den
                lse_acc[s] = lse_acc[s] + jnp.where(lane_l == head, m + jnp.log(den), 0.0)
        for s in range(n_sub):
            o_ref[s * BAND:(s + 1) * BAND, ls] = o_sub[s]
    for s in range(n_sub):
        l_ref[s * BAND:(s + 1) * BAND, :] = lse_acc[s]


def _swa_call(q, k, v, dil, batch, seq):
    sub_len = seq // dil
    tq = min(512, sub_len)
    nq = sub_len // tq
    rows = batch * seq // dil
    qv = q.reshape(rows, dil * GROUP_W)
    kv = k.reshape(rows, dil * GROUP_W)
    vv = v.reshape(rows, dil * GROUP_W)
    cur = lambda b, r, i: (b * nq + i, r)
    prv = lambda b, r, i: (jnp.maximum((b * nq + i) * (tq // BAND) - 1, 0), r)
    o, lse = pl.pallas_call(
        functools.partial(_swa_kernel, tq=tq),
        grid=(batch, dil, nq),
        in_specs=[
            pl.BlockSpec((tq, GROUP_W), cur),
            pl.BlockSpec((tq, GROUP_W), cur),
            pl.BlockSpec((BAND, GROUP_W), prv),
            pl.BlockSpec((tq, GROUP_W), cur),
            pl.BlockSpec((BAND, GROUP_W), prv),
        ],
        out_specs=[pl.BlockSpec((tq, GROUP_W), cur), pl.BlockSpec((tq, LANES), cur)],
        out_shape=[jax.ShapeDtypeStruct((rows, dil * GROUP_W), F32),
                   jax.ShapeDtypeStruct((rows, dil * LANES), F32)],
        compiler_params=pltpu.CompilerParams(dimension_semantics=("arbitrary",) * 3, vmem_limit_bytes=VMEM_LIMIT),
        name=f"swa_d{dil}",
    )(qv, kv, kv, vv, vv)
    return o.reshape(batch * seq, GROUP_W), lse.reshape(batch * seq, LANES)


def _swa_sample_kernel(q_ref, kn_ref, vn_ref, kc_ref, vc_ref, *out_refs, t_len, n_buf):
    nk_ref, nv_ref = out_refs[0], out_refs[1]
    o_refs = out_refs[2:2 + len(DILATIONS)]
    l_refs = out_refs[2 + len(DILATIONS):]
    kc = kc_ref[0]
    vc = vc_ref[0]
    k_new = kn_ref[0]
    v_new = vn_ref[0]
    nk_ref[0, 0:n_buf - t_len, :] = kc_ref[0, t_len:n_buf, :]
    nk_ref[0, n_buf - t_len:n_buf, :] = k_new
    nv_ref[0, 0:n_buf - t_len, :] = vc_ref[0, t_len:n_buf, :]
    nv_ref[0, n_buf - t_len:n_buf, :] = v_new

    rows = t_len * N_HEADS
    q_rep = jnp.broadcast_to(q_ref[0][:, None, :], (t_len, N_HEADS, GROUP_W)).reshape(rows, GROUP_W)
    rh = lax.broadcasted_iota(I32, (rows, GROUP_W), 0) & (N_HEADS - 1)
    hm = (lax.broadcasted_iota(I32, (rows, GROUP_W), 1) // HEAD_DIM) == rh
    q_bd = jnp.where(hm, q_rep * (HEAD_DIM ** -0.5), 0.0)
    pad_rows = LANES - t_len
    k_tail = jnp.concatenate([k_new, jnp.zeros((pad_rows, GROUP_W), F32)], axis=0)
    v_tail = jnp.concatenate([v_new, jnp.zeros((pad_rows, GROUP_W), F32)], axis=0)
    sc = jnp.concatenate([_mm_nt(q_bd, kc, True), _mm_nt(q_bd, k_tail, True)], axis=1)
    v_all = jnp.concatenate([vc, v_tail], axis=0)
    width = n_buf + LANES
    col = lax.broadcasted_iota(I32, (rows, width), 1)
    t_of_row = lax.broadcasted_iota(I32, (rows, width), 0) // N_HEADS
    delta = n_buf + t_of_row - col
    in_range = col < n_buf + t_len
    lane_l = lax.broadcasted_iota(I32, (rows, LANES), 1)
    rh_l = lax.broadcasted_iota(I32, (rows, LANES), 0) & (N_HEADS - 1)
    for bi, (window, dil) in enumerate(DILATIONS):
        valid = in_range & (delta >= 0) & ((delta & (dil - 1)) == 0) & (delta <= window)
        s_b = jnp.where(valid, sc, NEG_BIG)
        m = jnp.max(s_b, axis=1, keepdims=True)
        pm = jnp.exp(s_b - m)
        den = jnp.sum(pm, axis=1, keepdims=True)
        num = _mm(pm, v_all, True)
        o_rows = jnp.where(hm, num / den, 0.0)
        o_refs[bi][0] = jnp.sum(o_rows.reshape(t_len, N_HEADS, GROUP_W), axis=1)
        lse_rows = jnp.where(lane_l == rh_l, m + jnp.log(den), 0.0)
        l_refs[bi][0] = jnp.sum(lse_rows.reshape(t_len, N_HEADS, LANES), axis=1)


def _swa_sample_call(q, k_new, v_new, cache_k, cache_v):
    nbatch, t_len, _ = q.shape
    n_buf = cache_k.shape[1]
    blk = lambda b: (b, 0, 0)
    nd = len(DILATIONS)
    outs = pl.pallas_call(
        functools.partial(_swa_sample_kernel, t_len=t_len, n_buf=n_buf),
        grid=(nbatch,),
        in_specs=[pl.BlockSpec((1, t_len, GROUP_W), blk)] * 3 + [pl.BlockSpec((1, n_buf, GROUP_W), blk)] * 2,
        out_specs=([pl.BlockSpec((1, n_buf, GROUP_W), blk)] * 2 + [pl.BlockSpec((1, t_len, GROUP_W), blk)] * nd
                   + [pl.BlockSpec((1, t_len, LANES), blk)] * nd),
        out_shape=([jax.ShapeDtypeStruct((nbatch, n_buf, GROUP_W), F32)] * 2
                   + [jax.ShapeDtypeStruct((nbatch, t_len, GROUP_W), F32)] * nd
                   + [jax.ShapeDtypeStruct((nbatch, t_len, LANES), F32)] * nd),
        compiler_params=pltpu.CompilerParams(dimension_semantics=("arbitrary",), vmem_limit_bytes=VMEM_LIMIT),
        name="swa_sample",
    )(q, k_new, v_new, cache_k, cache_v)
    return outs[0], outs[1], outs[2:2 + nd], outs[2 + nd:]


def _outproj_kernel(*refs, route, n_dtype, precise):
    oa_ref = refs[0]
    o_refs = refs[1:4]
    l_refs = refs[4:7]
    x_ref, w_ref, g_ref = refs[7:10]
    if route:
        r_ref = refs[10]
        x1_ref, n_ref, ri_ref, rw_ref = refs[11:]
    else:
        x1_ref, n_ref = refs[10:]
    lses = [r[...] for r in l_refs]
    mx = jnp.maximum(jnp.maximum(lses[0], lses[1]), lses[2])
    es = [jnp.exp(l - mx) for l in lses]
    inv = 1.0 / (es[0] + es[1] + es[2])
    er = lax.broadcasted_iota(I32, (LANES, GROUP_W), 0)
    ec = lax.broadcasted_iota(I32, (LANES, GROUP_W), 1)
    expand = jnp.where(er == ec // HEAD_DIM, 1.0, 0.0).astype(BF16)
    ob = None
    for e, o_ref in zip(es, o_refs):
        hi, lo = _split2(e * inv)
        term = (_dot(hi, expand) + _dot(lo, expand)) * o_ref[...]
        ob = term if ob is None else ob + term
    mixed = _mm(oa_ref[...], w_ref[0:GROUP_W, :], precise) + _mm(ob, w_ref[GROUP_W:, :], precise)
    x1 = x_ref[...] + mixed
    x1_ref[...] = x1
    n = _rmsnorm(x1, g_ref[...])
    n_ref[...] = n.astype(n_dtype)
    if route:
        logits = jnp.dot(n, r_ref[...], preferred_element_type=F32, precision=lax.Precision.HIGHEST)
        lane = lax.broadcasted_iota(I32, logits.shape, 1)
        logits = jnp.where(lane < N_EXPERTS, logits, NEG_BIG)
        m1 = jnp.max(logits, axis=1, keepdims=True)
        i1 = jnp.min(jnp.where(logits == m1, lane, LANES), axis=1, keepdims=True)
        rest = jnp.where(lane == i1, NEG_BIG, logits)
        m2 = jnp.max(rest, axis=1, keepdims=True)
        i2 = jnp.min(jnp.where(rest == m2, lane, LANES), axis=1, keepdims=True)
        e2 = jnp.exp(m2 - m1)
        w1 = 1.0 / (1.0 + e2)
        w2 = e2 * w1
        ri_ref[...] = jnp.where(lane == 0, i1, jnp.where(lane == 1, i2, 0))
        rw_ref[...] = jnp.where(lane == 0, w1, jnp.where(lane == 1, w2, 0.0))


def _outproj_call(oa, o_br, l_br, x, w_out, g, router_pad, tm, n_dtype, precise=False):
    m = x.shape[0]
    route = router_pad is not None
    row = lambda i: (i, 0)
    const = lambda i: (0, 0)
    in_specs = ([pl.BlockSpec((tm, GROUP_W), row)] * 4 + [pl.BlockSpec((tm, LANES), row)] * 3
                + [pl.BlockSpec((tm, D_MODEL), row), pl.BlockSpec((D_MODEL, D_MODEL), const),
                   pl.BlockSpec((1, D_MODEL), const)])
    args = [oa, *o_br, *l_br, x, w_out, g]
    if route:
        in_specs.append(pl.BlockSpec((D_MODEL, LANES), const))
        args.append(router_pad)
    out_specs = [pl.BlockSpec((tm, D_MODEL), row)] * 2
    out_shape = [jax.ShapeDtypeStruct((m, D_MODEL), F32), jax.ShapeDtypeStruct((m, D_MODEL), n_dtype)]
    if route:
        out_specs += [pl.BlockSpec((tm, LANES), row)] * 2
        out_shape += [jax.ShapeDtypeStruct((m, LANES), I32), jax.ShapeDtypeStruct((m, LANES), F32)]
    return pl.pallas_call(
        functools.partial(_outproj_kernel, route=route, n_dtype=n_dtype, precise=precise),
        grid=(m // tm,),
        in_specs=in_specs,
        out_specs=out_specs,
        out_shape=out_shape,
        compiler_params=pltpu.CompilerParams(dimension_semantics=("arbitrary",), vmem_limit_bytes=VMEM_LIMIT),
        name="outproj",
    )(*args)


def _ffn_kernel(n_ref, x1_ref, wg_ref, wu_ref, wd_ref, o_ref, *, f_chunk, precise):
    n = n_ref[...]
    d_ff = wg_ref.shape[1]
    acc = x1_ref[...]
    for f0 in range(0, d_ff, f_chunk):
        h = _silu(_mm(n, wg_ref[:, f0:f0 + f_chunk], precise)) * _mm(n, wu_ref[:, f0:f0 + f_chunk], precise)
        acc = acc + _mm(h, wd_ref[f0:f0 + f_chunk, :], precise)
    o_ref[...] = acc


def _ffn_call(n, x1, wg, wu, wd, tm, precise=False):
    m = n.shape[0]
    d_ff = wg.shape[1]
    f_chunk = d_ff // 2 if (d_ff // 2) % LANES == 0 else d_ff
    row = lambda i: (i, 0)
    const = lambda i: (0, 0)
    return pl.pallas_call(
        functools.partial(_ffn_kernel, f_chunk=f_chunk, precise=precise),
        grid=(m // tm,),
        in_specs=[pl.BlockSpec((tm, D_MODEL), row), pl.BlockSpec((tm, D_MODEL), row),
                  pl.BlockSpec((D_MODEL, d_ff), const), pl.BlockSpec((D_MODEL, d_ff), const),
                  pl.BlockSpec((d_ff, D_MODEL), const)],
        out_specs=pl.BlockSpec((tm, D_MODEL), row),
        out_shape=jax.ShapeDtypeStruct((m, D_MODEL), F32),
        compiler_params=pltpu.CompilerParams(dimension_semantics=("arbitrary",), vmem_limit_bytes=VMEM_LIMIT),
        name="ffn_dense",
    )(n, x1, wg, wu, wd)


def _ffn_stream_kernel(n_ref, x1_ref, wg_ref, wu_ref, wd_ref, o_ref, *, precise):
    @pl.when(pl.program_id(0) == 0)
    def _():
        o_ref[...] = x1_ref[...]

    n = n_ref[...]
    h = _silu(_mm(n, wg_ref[...], precise)) * _mm(n, wu_ref[...], precise)
    o_ref[...] += _mm(h, wd_ref[...], precise)


def _ffn_stream_call(n, x1, wg, wu, wd, tf, precise):
    m = n.shape[0]
    d_ff = wg.shape[1]
    const = lambda j: (0, 0)
    return pl.pallas_call(
        functools.partial(_ffn_stream_kernel, precise=precise),
        grid=(d_ff // tf,),
        in_specs=[pl.BlockSpec((m, D_MODEL), const), pl.BlockSpec((m, D_MODEL), const),
                  pl.BlockSpec((D_MODEL, tf), lambda j: (0, j)), pl.BlockSpec((D_MODEL, tf), lambda j: (0, j)),
                  pl.BlockSpec((tf, D_MODEL), lambda j: (j, 0))],
        out_specs=pl.BlockSpec((m, D_MODEL), const),
        out_shape=jax.ShapeDtypeStruct((m, D_MODEL), F32),
        compiler_params=pltpu.CompilerParams(dimension_semantics=("arbitrary",), vmem_limit_bytes=VMEM_LIMIT),
        name="ffn_stream",
    )(n, x1, wg, wu, wd)


def _moe_kernel(te_ref, tv_ref, src_ref, n_hbm, wg_ref, wu_ref, wd_ref, y_ref, x_s, acc_s, sem, *, tme):
    i = pl.program_id(0)
    j = pl.program_id(1)
    nj = pl.num_programs(1)

    @pl.when(tv_ref[i] > 0)
    def _():
        @pl.when(j == 0)
        def _():
            def issue(r, c):
                tok = src_ref[0, 0, r]
                pltpu.make_async_copy(n_hbm.at[pl.ds(tok, 1), :], x_s.at[pl.ds(r, 1), :], sem).start()
                return c
            lax.fori_loop(0, tme, issue, 0)
            pltpu.make_async_copy(n_hbm.at[pl.ds(0, tme), :], x_s, sem).wait()
            acc_s[...] = jnp.zeros_like(acc_s)

        x = x_s[...].astype(BF16)
        h = _silu(_dot(x, wg_ref[0])) * _dot(x, wu_ref[0])
        acc_s[...] += _dot(h.astype(BF16), wd_ref[0])

        @pl.when(j == nj - 1)
        def _():
            y_ref[...] = acc_s[...]

    @pl.when((tv_ref[i] == 0) & (j == nj - 1))
    def _():
        y_ref[...] = jnp.zeros_like(y_ref)


def _moe_call(n_all, tile_expert, tile_valid, src_tok3, wg, wu, wd, tme, tf):
    n_tiles = tile_expert.shape[0]
    d_ff = wg.shape[2]
    nj = d_ff // tf

    def jsel(i, j, te, tv):
        return jnp.where(tv[i] > 0, j, nj - 1)

    grid_spec = pltpu.PrefetchScalarGridSpec(
        num_scalar_prefetch=2,
        grid=(n_tiles, nj),
        in_specs=[
            pl.BlockSpec((1, 1, tme), lambda i, j, te, tv: (i, 0, 0), memory_space=pltpu.SMEM),
            pl.BlockSpec(memory_space=pl.ANY),
            pl.BlockSpec((1, D_MODEL, tf), lambda i, j, te, tv: (te[i], 0, jsel(i, j, te, tv))),
            pl.BlockSpec((1, D_MODEL, tf), lambda i, j, te, tv: (te[i], 0, jsel(i, j, te, tv))),
            pl.BlockSpec((1, tf, D_MODEL), lambda i, j, te, tv: (te[i], jsel(i, j, te, tv), 0)),
        ],
        out_specs=pl.BlockSpec((tme, D_MODEL), lambda i, j, te, tv: (i, 0)),
        scratch_shapes=[pltpu.VMEM((tme, D_MODEL), F32), pltpu.VMEM((tme, D_MODEL), F32), pltpu.SemaphoreType.DMA],
    )
    return pl.pallas_call(
        functools.partial(_moe_kernel, tme=tme),
        grid_spec=grid_spec,
        out_shape=jax.ShapeDtypeStruct((n_tiles * tme, D_MODEL), F32),
        compiler_params=pltpu.CompilerParams(dimension_semantics=("arbitrary", "arbitrary"), vmem_limit_bytes=VMEM_LIMIT),
        name="moe_experts",
    )(tile_expert, tile_valid, src_tok3, n_all, wg, wu, wd)


def _route_plan(route_idx, tme, n_tiles):
    n_tok = route_idx.shape[0]
    e_flat = route_idx.reshape(-1)
    onehot = (e_flat[:, None] == jnp.arange(N_EXPERTS, dtype=I32)[None, :]).astype(I32)
    csum = jnp.cumsum(onehot, axis=0)
    rank = jnp.sum(csum * onehot, axis=1) - 1
    cnt = csum[-1]
    tiles_e = (cnt + tme - 1) // tme
    tile_end = jnp.cumsum(tiles_e)
    row_off = (tile_end - tiles_e) * tme
    dst = jnp.sum(onehot * row_off[None, :], axis=1) + rank
    tile_id = jnp.arange(n_tiles, dtype=I32)
    tile_expert = jnp.minimum(jnp.sum((tile_id[:, None] >= tile_end[None, :]).astype(I32), axis=1), N_EXPERTS - 1)
    tile_valid = (tile_id < tile_end[-1]).astype(I32)
    tok_of = jnp.arange(2 * n_tok, dtype=I32) // 2
    src_tok = jnp.zeros((n_tiles * tme,), I32).at[dst].set(tok_of)
    return dst.reshape(n_tok, 2), src_tok.reshape(n_tiles, 1, tme), tile_expert.astype(I32), tile_valid


def _moe_combine_kernel(pos_ref, y_hbm, x1_ref, rw_ref, g_ref, o_ref, y_s, sem, *, tm, final_norm):
    def issue(r, c):
        p0 = pos_ref[0, 0, 2 * r]
        p1 = pos_ref[0, 0, 2 * r + 1]
        pltpu.make_async_copy(y_hbm.at[pl.ds(p0, 1), :], y_s.at[0, pl.ds(r, 1), :], sem).start()
        pltpu.make_async_copy(y_hbm.at[pl.ds(p1, 1), :], y_s.at[1, pl.ds(r, 1), :], sem).start()
        return c
    lax.fori_loop(0, tm, issue, 0)
    for slot in range(2):
        pltpu.make_async_copy(y_hbm.at[pl.ds(0, tm), :], y_s.at[slot], sem).wait()
    rw = rw_ref[...]
    out = x1_ref[...] + rw[:, 0:1] * y_s[0] + rw[:, 1:2] * y_s[1]
    if final_norm:
        out = _rmsnorm(out, g_ref[...])
    o_ref[...] = out


def _moe_combine_call(pos3, y_sorted, x1, rw, g_final, tm, final_norm):
    m = x1.shape[0]
    row = lambda i: (i, 0)
    return pl.pallas_call(
        functools.partial(_moe_combine_kernel, tm=tm, final_norm=final_norm),
        grid=(m // tm,),
        in_specs=[
            pl.BlockSpec((1, 1, 2 * tm), lambda i: (i, 0, 0), memory_space=pltpu.SMEM),
            pl.BlockSpec(memory_space=pl.ANY),
            pl.BlockSpec((tm, D_MODEL), row),
            pl.BlockSpec((tm, LANES), row),
            pl.BlockSpec((1, D_MODEL), lambda i: (0, 0)),
        ],
        out_specs=pl.BlockSpec((tm, D_MODEL), row),
        out_shape=jax.ShapeDtypeStruct((m, D_MODEL), F32),
        scratch_shapes=[pltpu.VMEM((2, tm, D_MODEL), F32), pltpu.SemaphoreType.DMA],
        compiler_params=pltpu.CompilerParams(dimension_semantics=("arbitrary",), vmem_limit_bytes=VMEM_LIMIT),
        name="moe_combine",
    )(pos3, y_sorted, x1, rw, g_final)


def _final_norm_kernel(x_ref, g_ref, o_ref):
    o_ref[...] = _rmsnorm(x_ref[...], g_ref[...])


def _final_norm_call(x, g, tm):
    m = x.shape[0]
    row = lambda i: (i, 0)
    return pl.pallas_call(
        _final_norm_kernel,
        grid=(m // tm,),
        in_specs=[pl.BlockSpec((tm, D_MODEL), row), pl.BlockSpec((1, D_MODEL), lambda i: (0, 0))],
        out_specs=pl.BlockSpec((tm, D_MODEL), row),
        out_shape=jax.ShapeDtypeStruct((m, D_MODEL), F32),
        compiler_params=pltpu.CompilerParams(dimension_semantics=("arbitrary",)),
        name="final_norm",
    )(x, g)


def _rope_tables(pos):
    half = HEAD_DIM // 2
    inv_freq = np.power(np.float64(ROPE_THETA), -np.arange(half, dtype=np.float64) * (2.0 / HEAD_DIM))
    ang = np.asarray(pos, np.float64)[:, None] * inv_freq[None, :]
    c, s = np.cos(ang), np.sin(ang)
    cos_t = np.concatenate([c, c, c, c], axis=1).astype(np.float32)
    sin_t = np.concatenate([-s, s, -s, s], axis=1).astype(np.float32)
    return jnp.asarray(cos_t), jnp.asarray(sin_t)


def _pack_w_in(w):
    gates0 = QKV_W + GROUP_W
    pad = jnp.zeros((w.shape[0], LANES - 2 * N_HEADS), w.dtype)
    return jnp.concatenate([w[:, :gates0], w[:, gates0 + 2 * N_HEADS:], w[:, gates0:gates0 + 2 * N_HEADS], pad],
                           axis=1)


def _lane_row(vals, offset):
    return jnp.zeros((1, LANES), F32).at[0, offset:offset + vals.shape[0]].set(vals.astype(F32))


def _pad_rows(x, rows, front=0):
    return jnp.pad(x, ((0, 0), (front, rows - front - x.shape[1]), (0, 0)))


def kernel(x_prompt, x_sample, state_conv, state_delta, cache_win_k, cache_win_v, norm_mix, w_in, conv_w, a_log,
           dt_bias, gdn_norm, w_out, norm_ffn, ffn_gate, ffn_up, ffn_down, router, moe_gate, moe_up, moe_down,
           final_norm):
    bp, seq, _ = x_prompt.shape
    bs, t_len, _ = x_sample.shape
    depth = w_in.shape[0]
    n_p, n_s = bp * seq, bs * t_len
    n_all = n_p + n_s
    tm_p, tm_s = 512, n_s
    gdn_rows = 256
    n_buf = cache_win_k.shape[2]
    n_keep = min(WINDOW_MAX, seq)

    cos_p, sin_p = _rope_tables(np.arange(seq))
    cos_s, sin_s = _rope_tables(PAST_LEN + (np.arange(n_s) % t_len))
    xp = x_prompt.reshape(n_p, D_MODEL)
    xs = x_sample.reshape(n_s, D_MODEL)
    g_final = final_norm.reshape(1, D_MODEL)

    conv_p, delta_p, wk_p, wv_p = [], [], [], []
    conv_s, delta_s, wk_s, wv_s = [], [], [], []
    for l in range(depth):
        w_packed_f = _pack_w_in(w_in[l].astype(F32))
        w_packed = w_packed_f.astype(BF16)
        w_out_f = w_out[l].astype(F32)
        g_mix = norm_mix[l].reshape(1, D_MODEL)
        alog_row = _lane_row(a_log[l], N_HEADS)
        dtb_row = _lane_row(dt_bias[l], N_HEADS)
        gn_row = jnp.tile(gdn_norm[l].astype(F32), N_HEADS).reshape(1, GROUP_W)
        w_out_b = w_out[l].astype(BF16)
        g_ffn = norm_ffn[l].reshape(1, D_MODEL)
        is_moe = l % 2 == 1
        idx = l // 2

        qkv, z, gt, q, k, v = _inproj_call(xp, g_mix, w_packed, cos_p, sin_p, tm_p)
        oa, s_bd = _gdn_prompt_call(qkv, z, gt, conv_w[l], alog_row, dtb_row, gn_row, bp, seq, gdn_rows)
        branches = [_swa_call(q, k, v, dil, bp, seq) for _, dil in DILATIONS]
        conv_p.append(qkv.reshape(bp, seq, QKV_W)[:, seq - (CONV_TAPS - 1):])
        delta_p.append(_state_from_block_diag(s_bd))
        wk_p.append(k.reshape(bp, seq, N_HEADS, HEAD_DIM)[:, seq - n_keep:])
        wv_p.append(v.reshape(bp, seq, N_HEADS, HEAD_DIM)[:, seq - n_keep:])

        qkv_s, z_s, gt_s, q_s, k_s, v_s = _inproj_call(xs, g_mix, w_packed_f, cos_s, sin_s, tm_s, precise=True)
        prev_pad = _pad_rows(state_conv[l].astype(F32), 8, front=8 - (CONV_TAPS - 1))
        oa_s_pad, s_bd_s = _gdn_sample_call(
            _pad_rows(qkv_s.reshape(bs, t_len, QKV_W), CHUNK), prev_pad,
            _pad_rows(z_s.reshape(bs, t_len, GROUP_W), CHUNK), _pad_rows(gt_s.reshape(bs, t_len, LANES), CHUNK),
            conv_w[l], alog_row, dtb_row, gn_row, _state_to_block_diag(state_delta[l]), t_len, 4)
        oa_s = oa_s_pad[:, :t_len].reshape(n_s, GROUP_W)
        new_k, new_v, o_br_s, l_br_s = _swa_sample_call(
            q_s.reshape(bs, t_len, GROUP_W), k_s.reshape(bs, t_len, GROUP_W), v_s.reshape(bs, t_len, GROUP_W),
            cache_win_k[l].reshape(bs, n_buf, GROUP_W), cache_win_v[l].reshape(bs, n_buf, GROUP_W))
        conv_all = jnp.concatenate([state_conv[l].astype(F32), qkv_s.reshape(bs, t_len, QKV_W)], axis=1)
        conv_s.append(conv_all[:, conv_all.shape[1] - (CONV_TAPS - 1):])
        delta_s.append(_state_from_block_diag(s_bd_s))
        wk_s.append(new_k.reshape(bs, n_buf, N_HEADS, HEAD_DIM))
        wv_s.append(new_v.reshape(bs, n_buf, N_HEADS, HEAD_DIM))
        o_br_s = [o.reshape(n_s, GROUP_W) for o in o_br_s]
        l_br_s = [x.reshape(n_s, LANES) for x in l_br_s]

        o_br_p = [b[0] for b in branches]
        l_br_p = [b[1] for b in branches]
        if not is_moe:
            x1p, np_ = _outproj_call(oa.reshape(n_p, GROUP_W), o_br_p, l_br_p, xp, w_out_b, g_ffn, None, tm_p, BF16)
            x1s, ns_ = _outproj_call(oa_s, o_br_s, l_br_s, xs, w_out_f, g_ffn, None, tm_s, F32, precise=True)
            wg, wu, wd = ffn_gate[idx].astype(BF16), ffn_up[idx].astype(BF16), ffn_down[idx].astype(BF16)
            xp = _ffn_call(np_, x1p, wg, wu, wd, tm_p)
            xs = _ffn_stream_call(ns_, x1s, ffn_gate[idx].astype(F32), ffn_up[idx].astype(F32),
                                  ffn_down[idx].astype(F32), 256, precise=True)
            if l == depth - 1:
                xp = _final_norm_call(xp, g_final, tm_p)
                xs = _final_norm_call(xs, g_final, tm_s)
        else:
            router_pad = jnp.pad(router[idx].astype(F32), ((0, 0), (0, LANES - N_EXPERTS)))
            x1p, n_tok_p, ri_p, rw_p = _outproj_call(
                oa.reshape(n_p, GROUP_W), o_br_p, l_br_p, xp, w_out_b, g_ffn, router_pad, tm_p, F32)
            x1s, n_tok_s, ri_s, rw_s = _outproj_call(oa_s, o_br_s, l_br_s, xs, w_out_f, g_ffn, router_pad, tm_s, F32,
                                                     precise=True)
            n_tok = jnp.concatenate([n_tok_p, n_tok_s], axis=0)
            tme, tf = 512, 512
            n_tiles = (2 * n_all + N_EXPERTS * (tme - 1)) // tme
            ridx = jnp.concatenate([ri_p[:, :2], ri_s[:, :2]], axis=0)
            dst, src_tok3, tile_expert, tile_valid = _route_plan(ridx, tme, n_tiles)
            y_sorted = _moe_call(n_tok, tile_expert, tile_valid, src_tok3, moe_gate[idx].astype(BF16),
                                 moe_up[idx].astype(BF16), moe_down[idx].astype(BF16), tme, tf)
            last = l == depth - 1
            tmc_p, tmc_s = 256, n_s
            xp = _moe_combine_call(dst[:n_p].reshape(n_p // tmc_p, 1, 2 * tmc_p), y_sorted, x1p, rw_p, g_final,
                                   tmc_p, last)
            xs = _moe_combine_call(dst[n_p:].reshape(n_s // tmc_s, 1, 2 * tmc_s), y_sorted, x1s, rw_s, g_final,
                                   tmc_s, last)

    y_prompt = xp.reshape(bp, seq, D_MODEL)
    y_sample = xs.reshape(bs, t_len, D_MODEL)
    return (y_prompt, y_sample, jnp.stack(conv_p), jnp.stack(delta_p), jnp.stack(wk_p), jnp.stack(wv_p),
            jnp.stack(conv_s), jnp.stack(delta_s), jnp.stack(wk_s), jnp.stack(wv_s))
```

```python
import functools

import numpy as np
import jax
import jax.numpy as jnp
from jax import lax
from jax.experimental import pallas as pl
from jax.experimental.pallas import tpu as pltpu

F32 = jnp.float32
BF16 = jnp.bfloat16
I32 = jnp.int32

EPS = 1e-6
NEG_BIG = -1e30
D_MODEL = 1024
HEAD_DIM = 64
N_HEADS = 8
GROUP_W = N_HEADS * HEAD_DIM
QKV_W = 3 * GROUP_W
CONV_TAPS = 4
CHUNK = 64
HEADS_PER_PACK = 4
PACK_W = HEADS_PER_PACK * HEAD_DIM
DILATIONS = ((128, 1), (512, 4), (2048, 16))
BAND = 128
WINDOW_MAX = 2048
ROPE_THETA = 10000.0
PAST_LEN = 16384
N_EXPERTS = 8
PROJ_PACKED_W = QKV_W + 4 * GROUP_W + 128
LANES = 128
VMEM_LIMIT = 56 * 1024 * 1024


def _dot(a, b):
    return jnp.dot(a, b, preferred_element_type=F32)


def _dot_nt(a, b):
    return lax.dot_general(a, b, (((1,), (1,)), ((), ())), preferred_element_type=F32)


def _mm(a, b, precise=False):
    if precise:
        return jnp.dot(a.astype(F32), b.astype(F32), preferred_element_type=F32, precision=lax.Precision.HIGHEST)
    return jnp.dot(a.astype(BF16), b.astype(BF16), preferred_element_type=F32)


def _mm_nt(a, b, precise=False):
    if precise:
        return lax.dot_general(a.astype(F32), b.astype(F32), (((1,), (1,)), ((), ())), preferred_element_type=F32,
                               precision=lax.Precision.HIGHEST)
    return _dot_nt(a.astype(BF16), b.astype(BF16))


def _split2(x):
    hi = x.astype(BF16)
    lo = (x - hi.astype(F32)).astype(BF16)
    return hi, lo


def _split3(x):
    hi = x.astype(BF16)
    r = x - hi.astype(F32)
    mid = r.astype(BF16)
    lo = (r - mid.astype(F32)).astype(BF16)
    return hi, mid, lo


def _dot_x3(a, b):
    ah, al = _split2(a)
    bh, bl = _split2(b)
    m = a.shape[0]
    r = _dot(jnp.concatenate([ah, al], axis=0), bh)
    return r[:m] + r[m:] + _dot(ah, bl)


def _dot_exact_rhs01(x, e):
    return sum(_dot(p, e) for p in _split3(x))


def _sigmoid(x):
    return 1.0 / (1.0 + jnp.exp(-x))


def _silu(x):
    return x * _sigmoid(x)


def _softplus(x):
    return jnp.maximum(x, 0.0) + jnp.log1p(jnp.exp(-jnp.abs(x)))


def _rmsnorm(x, g):
    ms = jnp.mean(x * x, axis=-1, keepdims=True)
    return x * lax.rsqrt(ms + EPS) * g


def _head_ones(width):
    r = lax.broadcasted_iota(I32, (width, width), 0) // HEAD_DIM
    c = lax.broadcasted_iota(I32, (width, width), 1) // HEAD_DIM
    return jnp.where(r == c, 1.0, 0.0).astype(BF16)


def _head_sum(x, ones_bd):
    hi, lo = _split2(x)
    return _dot(hi, ones_bd) + _dot(lo, ones_bd)


def _rope(x, cos, sin):
    n = x.shape[1]
    reps = n // LANES
    c = jnp.concatenate([cos] * reps, axis=1)
    s = jnp.concatenate([sin] * reps, axis=1)
    lane = lax.broadcasted_iota(I32, x.shape, 1)
    first_half = (lane & (HEAD_DIM - 1)) < HEAD_DIM // 2
    partner = jnp.where(first_half, pltpu.roll(x, n - HEAD_DIM // 2, 1), pltpu.roll(x, HEAD_DIM // 2, 1))
    return x * c + partner * s


def _residue_perm(tm, dil):
    i = np.arange(tm)
    src = (i % (tm // dil)) * dil + i // (tm // dil)
    p = np.zeros((tm, tm), np.float32)
    p[i, src] = 1.0
    return p


def _inproj_kernel(x_ref, g_ref, w_ref, cos_ref, sin_ref, *rest, precise, dils):
    y = _rmsnorm(x_ref[...], g_ref[...])
    if not precise:
        y = y.astype(BF16)

    def mm(lo, hi):
        return _mm(y, w_ref[:, lo:hi], precise)

    cos = cos_ref[...]
    sin = sin_ref[...]
    n_perm = 0 if dils is None else sum(1 for d in dils if d > 1)
    perm_refs, outs = rest[:n_perm], rest[n_perm:]
    o = 0
    outs[0][...] = mm(o, o + QKV_W)
    o += QKV_W
    outs[1][...] = mm(o, o + GROUP_W)
    o += GROUP_W
    q = _rope(mm(o, o + GROUP_W), cos, sin)
    o += GROUP_W
    k = _rope(mm(o, o + GROUP_W), cos, sin)
    o += GROUP_W
    v = mm(o, o + GROUP_W)
    o += GROUP_W
    outs[2][...] = mm(o, o + LANES)
    if dils is None:
        outs[3][...] = q
        outs[4][...] = k
        outs[5][...] = v
        return
    outs[3][...] = k
    outs[4][...] = v
    tm = q.shape[0]
    qkv_b = [t.astype(BF16) for t in (q, k, v)]
    nxt, pi = 5, 0
    for d in dils:
        if d == 1:
            for t in qkv_b:
                outs[nxt][...] = t
                nxt += 1
            continue
        sub = tm // d
        for t in qkv_b:
            tp = _dot(perm_refs[pi][...], t).astype(BF16)
            for r in range(d):
                outs[nxt][:, r * GROUP_W:(r + 1) * GROUP_W] = tp[r * sub:(r + 1) * sub, :]
            nxt += 1
        pi += 1


def _inproj_call(x, g, w_packed, cos_tab, sin_tab, tm, precise=False, dils=None):
    m = x.shape[0]
    n_tab = cos_tab.shape[0] // tm
    row = lambda i: (i, 0)
    const = lambda i: (0, 0)
    tab = lambda i: (i % n_tab, 0)
    in_specs = [
        pl.BlockSpec((tm, D_MODEL), row),
        pl.BlockSpec((1, D_MODEL), const),
        pl.BlockSpec((D_MODEL, PROJ_PACKED_W), const),
        pl.BlockSpec((tm, LANES), tab),
        pl.BlockSpec((tm, LANES), tab),
    ]
    args = [x, g, w_packed, cos_tab, sin_tab]
    outs = [(m, QKV_W, tm, F32), (m, GROUP_W, tm, F32), (m, LANES, tm, F32)]
    if dils is None:
        outs += [(m, GROUP_W, tm, F32)] * 3
    else:
        outs += [(m, GROUP_W, tm, F32)] * 2
        for d in dils:
            outs += [(m // d, d * GROUP_W, tm // d, BF16)] * 3
            if d > 1:
                in_specs.append(pl.BlockSpec((tm, tm), const))
                args.append(jnp.asarray(_residue_perm(tm, d), BF16))
    return pl.pallas_call(
        functools.partial(_inproj_kernel, precise=precise, dils=dils),
        grid=(m // tm,),
        in_specs=in_specs,
        out_specs=[pl.BlockSpec((rows, w), row) for _, w, rows, _ in outs],
        out_shape=[jax.ShapeDtypeStruct((n, w), dt) for n, w, _, dt in outs],
        compiler_params=pltpu.CompilerParams(dimension_semantics=("arbitrary",), vmem_limit_bytes=VMEM_LIMIT),
        name="inproj",
    )(*args)


def _pack_masks():
    lane = lax.broadcasted_iota(I32, (1, PACK_W), 1)
    return [lane // HEAD_DIM == h for h in range(HEADS_PER_PACK)]


def _block_diag(x, head_masks):
    return jnp.concatenate([jnp.where(m, x, 0.0) for m in head_masks], axis=0)


def _gdn_chunk_consts():
    i = lax.broadcasted_iota(I32, (CHUNK, PACK_W), 0)
    j = lax.broadcasted_iota(I32, (CHUNK, PACK_W), 1) & (HEAD_DIM - 1)
    r = lax.broadcasted_iota(I32, (CHUNK, CHUNK), 0)
    c = lax.broadcasted_iota(I32, (CHUNK, CHUNK), 1)
    br = lax.broadcasted_iota(I32, (PACK_W, PACK_W), 0) // HEAD_DIM
    bc = lax.broadcasted_iota(I32, (PACK_W, PACK_W), 1) // HEAD_DIM
    return dict(
        head_masks=_pack_masks(),
        lower=i >= j,
        strict=i > j,
        upper_f=jnp.where(i <= j, 1.0, 0.0),
        eye=jnp.where(i == j, 1.0, 0.0),
        tri=jnp.where(r >= c, 1.0, 0.0).astype(BF16),
        block=br == bc,
    )


def _gdn_chunk(qn, kn, v, bw, gw, state, cst, precise):
    hm = cst["head_masks"]
    gcum = sum(_dot(cst["tri"], p) for p in _split3(gw))
    grow = jnp.sum(gw * cst["upper_f"], axis=0, keepdims=True)
    diff = gcum - grow
    decay = jnp.where(cst["lower"], jnp.exp(jnp.where(cst["lower"], diff, 0.0)), 0.0)
    kq = _mm_nt(jnp.concatenate([kn, qn], axis=0), _block_diag(kn, hm), precise)
    kk, qk = kq[:CHUNK], kq[CHUNK:]
    nmat = bw * kk * jnp.where(cst["strict"], decay, 0.0)
    p = -nmat
    y = cst["eye"] + p
    q = _dot_x3(p, _block_diag(p, hm))
    n_sq = CHUNK.bit_length() - 1
    for s in range(1, n_sq):
        q_bd = _block_diag(q, hm)
        if s < n_sq - 1:
            both = _dot_x3(jnp.concatenate([y, q], axis=0), q_bd)
            y = y + both[:CHUNK]
            q = both[CHUNK:]
        else:
            y = y + _dot_x3(y, q_bd)
    egc = jnp.exp(gcum)
    rhs = jnp.concatenate([_block_diag(bw * v, hm), _block_diag(bw * egc * kn, hm)], axis=1)
    sol = _mm(y, rhs, precise)
    u, wk = sol[:, :PACK_W], sol[:, PACK_W:]
    aqk = qk * decay
    qd = qn * egc
    glast = gcum[CHUNK - 1:CHUNK, :]
    kd = kn * jnp.exp(glast - gcum)
    m1 = _mm(jnp.concatenate([wk, qd], axis=0), state, precise)
    w = u - m1[:CHUNK]
    o = m1[CHUNK:] + _mm(aqk, _block_diag(w, hm), precise)
    inc = _mm(jnp.transpose(kd), w, precise)
    new_state = state * jnp.exp(glast) + jnp.where(cst["block"], inc, 0.0)
    return o, new_state


def _bmm(a, b, precise=False):
    dn = (((2,), (1,)), ((0,), (0,)))
    if precise:
        return lax.dot_general(a.astype(F32), b.astype(F32), dn, preferred_element_type=F32,
                               precision=lax.Precision.HIGHEST)
    return lax.dot_general(a.astype(BF16), b.astype(BF16), dn, preferred_element_type=F32)


def _bmm_nt(a, b, precise=False):
    dn = (((2,), (2,)), ((0,), (0,)))
    if precise:
        return lax.dot_general(a.astype(F32), b.astype(F32), dn, preferred_element_type=F32,
                               precision=lax.Precision.HIGHEST)
    return lax.dot_general(a.astype(BF16), b.astype(BF16), dn, preferred_element_type=F32)


def _block_diag_b(x, head_masks):
    return jnp.concatenate([jnp.where(m, x, 0.0) for m in head_masks], axis=1)


def _gdn_local(qn, kn, v, bw, gw, cst, precise):
    hm = cst["head_masks"]
    nb = qn.shape[0]
    tri = jnp.broadcast_to(cst["tri"], (nb, CHUNK, CHUNK))
    gcum = sum(_bmm(tri, p) for p in _split3(gw))
    grow = jnp.sum(gw * cst["upper_f"], axis=1, keepdims=True)
    diff = gcum - grow
    decay = jnp.where(cst["lower"], jnp.exp(jnp.where(cst["lower"], diff, 0.0)), 0.0)
    kq = _bmm_nt(jnp.concatenate([kn, qn], axis=1), _block_diag_b(kn, hm), precise)
    kk, qk = kq[:, :CHUNK], kq[:, CHUNK:]
    nmat = bw * kk * jnp.where(cst["strict"], decay, 0.0)
    p = -nmat
    y = cst["eye"] + p
    q = _bmm(p, _block_diag_b(p, hm), precise)
    n_sq = CHUNK.bit_length() - 1
    for s in range(1, n_sq):
        q_bd = _block_diag_b(q, hm)
        if s < n_sq - 1:
            both = _bmm(jnp.concatenate([y, q], axis=1), q_bd, precise)
            y = y + both[:, :CHUNK]
            q = both[:, CHUNK:]
        else:
            y = y + _bmm(y, q_bd, precise)
    egc = jnp.exp(gcum)
    rhs = jnp.concatenate([_block_diag_b(bw * v, hm), _block_diag_b(bw * egc * kn, hm)], axis=2)
    sol = _bmm(y, rhs, precise)
    glast = gcum[:, CHUNK - 1:CHUNK, :]
    return (sol[:, :, :PACK_W], sol[:, :, PACK_W:], qk * decay, qn * egc, kn * jnp.exp(glast - gcum),
            jnp.exp(glast))


def _gdn_scan_step(u, wk, aqk, qd, kd, gl, state, cst, precise):
    hm = cst["head_masks"]
    m1 = _bmm(jnp.concatenate([wk, qd], axis=1), state, precise)
    w = u - m1[:, :CHUNK]
    o = m1[:, CHUNK:] + _bmm(aqk, _block_diag_b(w, hm), precise)
    inc = _bmm(jnp.swapaxes(kd, 1, 2), w, precise)
    return o, state * gl + jnp.where(cst["block"], inc, 0.0)


def _gdn_kernel(*refs, nb, rows, carry, n_valid, precise):
    if carry:
        (qkv_ref, prev_ref, z_ref, gt_ref, cw_ref, al_ref, db_ref, gn_ref, o_ref, sout_ref, xp_s, st_s) = refs
        s0_ref = None
    else:
        (qkv_ref, prev_ref, z_ref, gt_ref, cw_ref, al_ref, db_ref, gn_ref, s0_ref, o_ref, sout_ref, xp_s, st_s) = refs
    step = pl.program_id(0)
    ones_bd = _head_ones(GROUP_W)
    lane = lax.broadcasted_iota(I32, (1, LANES), 1)
    er = lax.broadcasted_iota(I32, (LANES, 2 * GROUP_W), 0)
    ec = lax.broadcasted_iota(I32, (LANES, 2 * GROUP_W), 1)
    expand = jnp.where(((ec < GROUP_W) & (er == ec // HEAD_DIM))
                       | ((ec >= GROUP_W) & (er == N_HEADS + (ec - GROUP_W) // HEAD_DIM)), 1.0, 0.0).astype(BF16)
    cw = cw_ref[...]

    if carry:
        @pl.when(step == 0)
        def _():
            st_s[...] = jnp.zeros_like(st_s)
    else:
        st_s[...] = s0_ref[...].reshape(st_s.shape)

    per_seq = []
    for b in range(nb):
        prev = prev_ref[b]
        if carry:
            prev = jnp.where(step == 0, 0.0, prev)
        xp_s[b, 0:8, :] = prev
        xp_s[b, 8:8 + rows, :] = qkv_ref[b]
        acc = cw[CONV_TAPS - 1:CONV_TAPS, :] * xp_s[b, 8:8 + rows, :]
        for t in range(1, CONV_TAPS):
            acc = acc + cw[CONV_TAPS - 1 - t:CONV_TAPS - t, :] * xp_s[b, 8 - t:8 - t + rows, :]
        cq = _silu(acc)
        q = cq[:, :GROUP_W]
        k = cq[:, GROUP_W:2 * GROUP_W]
        v = cq[:, 2 * GROUP_W:]
        qn = q * lax.rsqrt(_head_sum(q * q, ones_bd) + EPS) * (HEAD_DIM ** -0.5)
        kn = k * lax.rsqrt(_head_sum(k * k, ones_bd) + EPS)
        t_raw = gt_ref[b]
        beta = _sigmoid(t_raw)
        gdec = -jnp.exp(al_ref[...]) * _softplus(t_raw + db_ref[...])
        wide = _dot_exact_rhs01(jnp.where(lane < N_HEADS, beta, gdec), expand)
        bw = wide[:, :GROUP_W]
        gw = wide[:, GROUP_W:]
        if n_valid < rows:
            live = lax.broadcasted_iota(I32, (rows, 1), 0) < n_valid
            bw = jnp.where(live, bw, 0.0)
            gw = jnp.where(live, gw, 0.0)
            qn = jnp.where(live, qn, 0.0)
            kn = jnp.where(live, kn, 0.0)
            v = jnp.where(live, v, 0.0)
        per_seq.append((qn, kn, v, bw, gw))

    cst = _gdn_chunk_consts()
    n_chunks = rows // CHUNK
    n_pack = N_HEADS // HEADS_PER_PACK

    def instances(idx):
        return jnp.stack([per_seq[b][idx][c * CHUNK:(c + 1) * CHUNK, g * PACK_W:(g + 1) * PACK_W]
                          for c in range(n_chunks) for b in range(nb) for g in range(n_pack)], axis=0)

    u, wk, aqk, qd, kd, gl = _gdn_local(*(instances(i) for i in range(5)), cst, precise)
    n_chain = nb * n_pack
    state = st_s[...]
    o_chunks = []
    for c in range(n_chunks):
        sl = slice(c * n_chain, (c + 1) * n_chain)
        o_c, state = _gdn_scan_step(u[sl], wk[sl], aqk[sl], qd[sl], kd[sl], gl[sl], state, cst, precise)
        o_chunks.append(o_c)
    st_s[...] = state

    for b in range(nb):
        o = jnp.concatenate(
            [jnp.concatenate([o_chunks[c][b * n_pack + g] for g in range(n_pack)], axis=1) for c in range(n_chunks)],
            axis=0)
        ms = _head_sum(o * o, ones_bd) * (1.0 / HEAD_DIM)
        o_ref[b] = (o * lax.rsqrt(ms + EPS) * gn_ref[...] * _silu(z_ref[b])).astype(o_ref.dtype)

    if carry:
        @pl.when(step == pl.num_programs(0) - 1)
        def _():
            sout_ref[...] = state.reshape(sout_ref.shape)
    else:
        sout_ref[...] = state.reshape(sout_ref.shape)


def _gdn_scratch(nb, rows):
    return [pltpu.VMEM((nb, rows + 8, QKV_W), F32),
            pltpu.VMEM((nb * (N_HEADS // HEADS_PER_PACK), PACK_W, PACK_W), F32)]


def _gdn_prompt_call(qkv, z, gt, conv_w, alog_row, dtb_row, gn_row, batch, seq, rows):
    qkv3 = qkv.reshape(batch, seq, QKV_W)
    z3 = z.reshape(batch, seq, GROUP_W)
    gt3 = gt.reshape(batch, seq, LANES)
    blk = lambda i: (0, i, 0)
    const2 = lambda i: (0, 0)
    n_pack = N_HEADS // HEADS_PER_PACK
    return pl.pallas_call(
        functools.partial(_gdn_kernel, nb=batch, rows=rows, carry=True, n_valid=rows, precise=False),
        grid=(seq // rows,),
        in_specs=[
            pl.BlockSpec((batch, rows, QKV_W), blk),
            pl.BlockSpec((batch, 8, QKV_W), lambda i: (0, jnp.maximum(i * (rows // 8) - 1, 0), 0)),
            pl.BlockSpec((batch, rows, GROUP_W), blk),
            pl.BlockSpec((batch, rows, LANES), blk),
            pl.BlockSpec((CONV_TAPS, QKV_W), const2),
            pl.BlockSpec((1, LANES), const2),
            pl.BlockSpec((1, LANES), const2),
            pl.BlockSpec((1, GROUP_W), const2),
        ],
        out_specs=[
            pl.BlockSpec((batch, rows, GROUP_W), blk),
            pl.BlockSpec((batch, n_pack, PACK_W, PACK_W), lambda i: (0, 0, 0, 0)),
        ],
        out_shape=[
            jax.ShapeDtypeStruct((batch, seq, GROUP_W), BF16),
            jax.ShapeDtypeStruct((batch, n_pack, PACK_W, PACK_W), F32),
        ],
        scratch_shapes=_gdn_scratch(batch, rows),
        compiler_params=pltpu.CompilerParams(dimension_semantics=("arbitrary",), vmem_limit_bytes=VMEM_LIMIT),
        name="gdn_prompt",
    )(qkv3, qkv3, z3, gt3, conv_w, alog_row, dtb_row, gn_row)


def _gdn_sample_call(qkv_pad, prev_pad, z_pad, gt_pad, conv_w, alog_row, dtb_row, gn_row, s0_bd, n_valid, nb):
    n_seq = qkv_pad.shape[0]
    blk = lambda i: (i, 0, 0)
    blk4 = lambda i: (i, 0, 0, 0)
    const2 = lambda i: (0, 0)
    n_pack = N_HEADS // HEADS_PER_PACK
    return pl.pallas_call(
        functools.partial(_gdn_kernel, nb=nb, rows=CHUNK, carry=False, n_valid=n_valid, precise=True),
        grid=(n_seq // nb,),
        in_specs=[
            pl.BlockSpec((nb, CHUNK, QKV_W), blk),
            pl.BlockSpec((nb, 8, QKV_W), blk),
            pl.BlockSpec((nb, CHUNK, GROUP_W), blk),
            pl.BlockSpec((nb, CHUNK, LANES), blk),
            pl.BlockSpec((CONV_TAPS, QKV_W), const2),
            pl.BlockSpec((1, LANES), const2),
            pl.BlockSpec((1, LANES), const2),
            pl.BlockSpec((1, GROUP_W), const2),
            pl.BlockSpec((nb, n_pack, PACK_W, PACK_W), blk4),
        ],
        out_specs=[
            pl.BlockSpec((nb, CHUNK, GROUP_W), blk),
            pl.BlockSpec((nb, n_pack, PACK_W, PACK_W), blk4),
        ],
        out_shape=[
            jax.ShapeDtypeStruct((n_seq, CHUNK, GROUP_W), F32),
            jax.ShapeDtypeStruct((n_seq, n_pack, PACK_W, PACK_W), F32),
        ],
        scratch_shapes=_gdn_scratch(nb, CHUNK),
        compiler_params=pltpu.CompilerParams(dimension_semantics=("arbitrary",), vmem_limit_bytes=VMEM_LIMIT),
        name="gdn_sample",
    )(qkv_pad, prev_pad, z_pad, gt_pad, conv_w, alog_row, dtb_row, gn_row, s0_bd)


def _state_to_block_diag(s):
    n = s.shape[0]
    n_pack = N_HEADS // HEADS_PER_PACK
    s5 = s.reshape(n, n_pack, HEADS_PER_PACK, HEAD_DIM, HEAD_DIM).astype(F32)
    eye = jnp.eye(HEADS_PER_PACK, dtype=F32)
    return jnp.einsum("bghde,hk->bghdke", s5, eye).reshape(n, n_pack, PACK_W, PACK_W)


def _state_from_block_diag(s_bd):
    n = s_bd.shape[0]
    n_pack = N_HEADS // HEADS_PER_PACK
    s6 = s_bd.reshape(n, n_pack, HEADS_PER_PACK, HEAD_DIM, HEADS_PER_PACK, HEAD_DIM)
    diag = jnp.stack([s6[:, :, h, :, h, :] for h in range(HEADS_PER_PACK)], axis=2)
    return diag.reshape(n, N_HEADS, HEAD_DIM, HEAD_DIM)


def _swa_kernel(q_ref, kc_ref, kp_ref, vc_ref, vp_ref, o_ref, l_ref, *, tq):
    first_block = pl.program_id(2) == 0
    n_sub = tq // BAND
    lane = lax.broadcasted_iota(I32, (1, LANES), 1)
    low = lane < HEAD_DIM
    qi = lax.broadcasted_iota(I32, (BAND, 2 * BAND), 0)
    kj = lax.broadcasted_iota(I32, (BAND, 2 * BAND), 1)
    dist = qi + BAND - kj
    band = (dist >= 0) & (dist <= BAND)
    band_first = band & ((kj >= BAND) | jnp.logical_not(first_block))
    lane_l = lax.broadcasted_iota(I32, (BAND, LANES), 1)
    lse_acc = [jnp.zeros((BAND, LANES), F32) for _ in range(n_sub)]
    for p in range(GROUP_W // LANES):
        ls = slice(p * LANES, (p + 1) * LANES)
        q_pair = (q_ref[:, ls].astype(F32) * (HEAD_DIM ** -0.5)).astype(BF16)
        k_all = jnp.concatenate([kp_ref[:, ls], kc_ref[:, ls]], axis=0)
        v_all = jnp.concatenate([vp_ref[:, ls], vc_ref[:, ls]], axis=0)
        o_sub = [jnp.zeros((BAND, LANES), F32) for _ in range(n_sub)]
        for hh in range(2):
            hmask = low if hh == 0 else jnp.logical_not(low)
            k_h = jnp.where(hmask, k_all, jnp.zeros_like(k_all))
            v_h = jnp.where(hmask, v_all, jnp.zeros_like(v_all))
            head = 2 * p + hh
            for s in range(n_sub):
                sc = _dot_nt(q_pair[s * BAND:(s + 1) * BAND], k_h[s * BAND:s * BAND + 2 * BAND])
                valid = band_first if s == 0 else band
                sc = jnp.where(valid, sc, NEG_BIG)
                m = jnp.max(sc, axis=1, keepdims=True)
                pm = jnp.exp(sc - m)
                den = jnp.sum(pm, axis=1, keepdims=True)
                num = _dot(pm.astype(BF16), v_h[s * BAND:s * BAND + 2 * BAND])
                o_sub[s] = o_sub[s] + num / den
                lse_acc[s] = lse_acc[s] + jnp.where(lane_l == head, m + jnp.log(den), 0.0)
        for s in range(n_sub):
            o_ref[s * BAND:(s + 1) * BAND, ls] = o_sub[s].astype(o_ref.dtype)
    for s in range(n_sub):
        l_ref[s * BAND:(s + 1) * BAND, :] = lse_acc[s]


def _swa_call(qv, kv, vv, dil, batch, seq):
    sub_len = seq // dil
    tq = min(512, sub_len)
    nq = sub_len // tq
    rows = batch * seq // dil
    cur = lambda b, r, i: (b * nq + i, r)
    prv = lambda b, r, i: (jnp.maximum((b * nq + i) * (tq // BAND) - 1, 0), r)
    return pl.pallas_call(
        functools.partial(_swa_kernel, tq=tq),
        grid=(batch, dil, nq),
        in_specs=[
            pl.BlockSpec((tq, GROUP_W), cur),
            pl.BlockSpec((tq, GROUP_W), cur),
            pl.BlockSpec((BAND, GROUP_W), prv),
            pl.BlockSpec((tq, GROUP_W), cur),
            pl.BlockSpec((BAND, GROUP_W), prv),
        ],
        out_specs=[pl.BlockSpec((tq, GROUP_W), cur), pl.BlockSpec((tq, LANES), cur)],
        out_shape=[jax.ShapeDtypeStruct((rows, dil * GROUP_W), BF16),
                   jax.ShapeDtypeStruct((rows, dil * LANES), F32)],
        compiler_params=pltpu.CompilerParams(dimension_semantics=("arbitrary",) * 3, vmem_limit_bytes=VMEM_LIMIT),
        name=f"swa_d{dil}",
    )(qv, kv, kv, vv, vv)


def _swa_sample_kernel(*refs, t_len, n_buf, n_alias):
    q_ref, kn_ref, vn_ref, kc_ref, vc_ref = refs[:5]
    nk_ref, nv_ref, o_ref, l_ref = refs[5 + n_alias:]
    keep = n_buf - t_len
    nk_ref[0:keep] = kc_ref[t_len:n_buf]
    nk_ref[keep:n_buf] = kn_ref[...]
    nv_ref[0:keep] = vc_ref[t_len:n_buf]
    nv_ref[keep:n_buf] = vn_ref[...]
    for t in range(t_len):
        q = q_ref[t] * (HEAD_DIM ** -0.5)
        for bi, (window, dil) in enumerate(DILATIONS):
            band = window // dil
            start = keep + t - (band - 1) * dil
            far = n_buf + t - band * dil
            assert start >= 0 and 0 <= far < n_buf
            ka = nk_ref[pl.ds(start, band, stride=dil)]
            va = nv_ref[pl.ds(start, band, stride=dil)]
            kx = kc_ref[far:far + 1]
            vx = vc_ref[far:far + 1]
            s_a = jnp.sum(ka * q[None], axis=-1, keepdims=True)
            s_x = jnp.sum(kx * q[None], axis=-1, keepdims=True)
            m = jnp.maximum(jnp.max(s_a, axis=0, keepdims=True), s_x)
            p_a = jnp.exp(s_a - m)
            p_x = jnp.exp(s_x - m)
            den = jnp.sum(p_a, axis=0, keepdims=True) + p_x
            num = jnp.sum(p_a * va, axis=0, keepdims=True) + p_x * vx
            o_ref[bi * t_len + t] = (num / den)[0]
            l_ref[bi * t_len + t] = jnp.broadcast_to((m + jnp.log(den))[0], (N_HEADS, LANES))


def _swa_sample_call(q, k_new, v_new, cache_k, cache_v, layer, new_k_all, new_v_all):
    nbatch, t_len = q.shape[:2]
    depth, _, n_buf = cache_k.shape[:3]
    nd = len(DILATIONS)
    tok = pl.BlockSpec((None, t_len, N_HEADS, HEAD_DIM), lambda b: (b, 0, 0, 0))
    cache = pl.BlockSpec((None, None, n_buf, N_HEADS, HEAD_DIM), lambda b: (layer, b, 0, 0, 0))
    cache_out = pl.BlockSpec((None, None, n_buf, N_HEADS, HEAD_DIM), lambda b: (layer, b, 0, 0, 0),
                             pipeline_mode=pl.Buffered(1))
    in_specs = [tok, tok, tok, cache, cache]
    args = [q, k_new, v_new, cache_k, cache_v]
    aliases = {}
    if new_k_all is not None:
        in_specs += [pl.BlockSpec(memory_space=pl.ANY)] * 2
        args += [new_k_all, new_v_all]
        aliases = {5: 0, 6: 1}
    return pl.pallas_call(
        functools.partial(_swa_sample_kernel, t_len=t_len, n_buf=n_buf, n_alias=len(aliases)),
        grid=(nbatch,),
        in_specs=in_specs,
        out_specs=[cache_out, cache_out,
                   pl.BlockSpec((None, nd * t_len, N_HEADS, HEAD_DIM), lambda b: (b, 0, 0, 0)),
                   pl.BlockSpec((None, nd * t_len, N_HEADS, LANES), lambda b: (b, 0, 0, 0))],
        out_shape=[jax.ShapeDtypeStruct(cache_k.shape, F32), jax.ShapeDtypeStruct(cache_v.shape, F32),
                   jax.ShapeDtypeStruct((nbatch, nd * t_len, N_HEADS, HEAD_DIM), F32),
                   jax.ShapeDtypeStruct((nbatch, nd * t_len, N_HEADS, LANES), F32)],
        input_output_aliases=aliases,
        compiler_params=pltpu.CompilerParams(dimension_semantics=("arbitrary",), vmem_limit_bytes=VMEM_LIMIT),
        name="swa_sample",
    )(*args)


def _outproj_kernel(*refs, route, n_dtype, precise, dils):
    oa_ref = refs[0]
    o_refs = refs[1:4]
    l_refs = refs[4:7]
    x_ref, w_ref, g_ref = refs[7:10]
    nxt = 10
    if route:
        r_ref = refs[nxt]
        nxt += 1
    n_perm = sum(1 for d in dils if d > 1)
    perm_refs = refs[nxt:nxt + n_perm]
    nxt += n_perm
    if route:
        x1_ref, n_ref, ri_ref, rw_ref = refs[nxt:]
    else:
        x1_ref, n_ref = refs[nxt:]

    def token_order(ref, d, pi, width, exact_f32):
        if d == 1:
            return ref[...]
        stacked = jnp.concatenate([ref[:, r * width:(r + 1) * width] for r in range(d)], axis=0)
        pt = perm_refs[pi][...]
        if exact_f32:
            return sum(_dot(pt, part) for part in _split3(stacked))
        return _dot(pt, stacked)

    o_vals, lses, pi = [], [], 0
    for d, o_ref, l_ref in zip(dils, o_refs, l_refs):
        o_vals.append(token_order(o_ref, d, pi, GROUP_W, False))
        lses.append(token_order(l_ref, d, pi, LANES, True))
        pi += 1 if d > 1 else 0
    mx = jnp.maximum(jnp.maximum(lses[0], lses[1]), lses[2])
    es = [jnp.exp(l - mx) for l in lses]
    inv = 1.0 / (es[0] + es[1] + es[2])
    er = lax.broadcasted_iota(I32, (LANES, GROUP_W), 0)
    ec = lax.broadcasted_iota(I32, (LANES, GROUP_W), 1)
    expand = jnp.where(er == ec // HEAD_DIM, 1.0, 0.0).astype(BF16)
    ob = None
    for e, o_val in zip(es, o_vals):
        hi, lo = _split2(e * inv)
        term = (_dot(hi, expand) + _dot(lo, expand)) * o_val
        ob = term if ob is None else ob + term
    mixed = _mm(oa_ref[...], w_ref[0:GROUP_W, :], precise) + _mm(ob, w_ref[GROUP_W:, :], precise)
    x1 = x_ref[...] + mixed
    x1_ref[...] = x1
    n = _rmsnorm(x1, g_ref[...])
    n_ref[...] = n.astype(n_dtype)
    if route:
        logits = _mm(n, r_ref[...], True) if precise else _dot_x3(n, r_ref[...])
        lane = lax.broadcasted_iota(I32, logits.shape, 1)
        logits = jnp.where(lane < N_EXPERTS, logits, NEG_BIG)
        m1 = jnp.max(logits, axis=1, keepdims=True)
        i1 = jnp.min(jnp.where(logits == m1, lane, LANES), axis=1, keepdims=True)
        rest = jnp.where(lane == i1, NEG_BIG, logits)
        m2 = jnp.max(rest, axis=1, keepdims=True)
        i2 = jnp.min(jnp.where(rest == m2, lane, LANES), axis=1, keepdims=True)
        e2 = jnp.exp(m2 - m1)
        w1 = 1.0 / (1.0 + e2)
        w2 = e2 * w1
        ri_ref[...] = jnp.where(lane == 0, i1, jnp.where(lane == 1, i2, 0))
        rw_ref[...] = jnp.where(lane == 0, w1, jnp.where(lane == 1, w2, 0.0))


def _outproj_call(oa, o_br, l_br, x, w_out, g, router_pad, tm, n_dtype, precise=False, dils=(1, 1, 1)):
    m = x.shape[0]
    route = router_pad is not None
    row = lambda i: (i, 0)
    const = lambda i: (0, 0)
    in_specs = ([pl.BlockSpec((tm, GROUP_W), row)]
                + [pl.BlockSpec((tm // d, d * GROUP_W), row) for d in dils]
                + [pl.BlockSpec((tm // d, d * LANES), row) for d in dils]
                + [pl.BlockSpec((tm, D_MODEL), row), pl.BlockSpec((D_MODEL, D_MODEL), const),
                   pl.BlockSpec((1, D_MODEL), const)])
    args = [oa, *o_br, *l_br, x, w_out, g]
    if route:
        in_specs.append(pl.BlockSpec((D_MODEL, LANES), const))
        args.append(router_pad)
    for d in dils:
        if d > 1:
            in_specs.append(pl.BlockSpec((tm, tm), const))
            args.append(jnp.asarray(_residue_perm(tm, d).T, BF16))
    out_specs = [pl.BlockSpec((tm, D_MODEL), row)] * 2
    out_shape = [jax.ShapeDtypeStruct((m, D_MODEL), F32), jax.ShapeDtypeStruct((m, D_MODEL), n_dtype)]
    if route:
        out_specs += [pl.BlockSpec((tm, LANES), row)] * 2
        out_shape += [jax.ShapeDtypeStruct((m, LANES), I32), jax.ShapeDtypeStruct((m, LANES), F32)]
    return pl.pallas_call(
        functools.partial(_outproj_kernel, route=route, n_dtype=n_dtype, precise=precise, dils=tuple(dils)),
        grid=(m // tm,),
        in_specs=in_specs,
        out_specs=out_specs,
        out_shape=out_shape,
        compiler_params=pltpu.CompilerParams(dimension_semantics=("arbitrary",), vmem_limit_bytes=VMEM_LIMIT),
        name="outproj",
    )(*args)


def _ffn_kernel(n_ref, x1_ref, wg_ref, wu_ref, wd_ref, o_ref, *, f_chunk, precise):
    n = n_ref[...]
    d_ff = wg_ref.shape[1]
    acc = x1_ref[...]
    for f0 in range(0, d_ff, f_chunk):
        h = _silu(_mm(n, wg_ref[:, f0:f0 + f_chunk], precise)) * _mm(n, wu_ref[:, f0:f0 + f_chunk], precise)
        acc = acc + _mm(h, wd_ref[f0:f0 + f_chunk, :], precise)
    o_ref[...] = acc


def _ffn_call(n, x1, wg, wu, wd, tm, precise=False):
    m = n.shape[0]
    d_ff = wg.shape[1]
    f_chunk = d_ff // 2 if (d_ff // 2) % LANES == 0 else d_ff
    row = lambda i: (i, 0)
    const = lambda i: (0, 0)
    return pl.pallas_call(
        functools.partial(_ffn_kernel, f_chunk=f_chunk, precise=precise),
        grid=(m // tm,),
        in_specs=[pl.BlockSpec((tm, D_MODEL), row), pl.BlockSpec((tm, D_MODEL), row),
                  pl.BlockSpec((D_MODEL, d_ff), const), pl.BlockSpec((D_MODEL, d_ff), const),
                  pl.BlockSpec((d_ff, D_MODEL), const)],
        out_specs=pl.BlockSpec((tm, D_MODEL), row),
        out_shape=jax.ShapeDtypeStruct((m, D_MODEL), F32),
        compiler_params=pltpu.CompilerParams(dimension_semantics=("arbitrary",), vmem_limit_bytes=VMEM_LIMIT),
        name="ffn_dense",
    )(n, x1, wg, wu, wd)


def _ffn_stream_kernel(n_ref, x1_ref, wg_ref, wu_ref, wd_ref, o_ref, *, precise):
    @pl.when(pl.program_id(0) == 0)
    def _():
        o_ref[...] = x1_ref[...]

    n = n_ref[...]
    h = _silu(_mm(n, wg_ref[...], precise)) * _mm(n, wu_ref[...], precise)
    o_ref[...] += _mm(h, wd_ref[...], precise)


def _ffn_stream_call(n, x1, wg, wu, wd, tf, precise):
    m = n.shape[0]
    d_ff = wg.shape[1]
    const = lambda j: (0, 0)
    return pl.pallas_call(
        functools.partial(_ffn_stream_kernel, precise=precise),
        grid=(d_ff // tf,),
        in_specs=[pl.BlockSpec((m, D_MODEL), const), pl.BlockSpec((m, D_MODEL), const),
                  pl.BlockSpec((D_MODEL, tf), lambda j: (0, j)), pl.BlockSpec((D_MODEL, tf), lambda j: (0, j)),
                  pl.BlockSpec((tf, D_MODEL), lambda j: (j, 0))],
        out_specs=pl.BlockSpec((m, D_MODEL), const),
        out_shape=jax.ShapeDtypeStruct((m, D_MODEL), F32),
        compiler_params=pltpu.CompilerParams(dimension_semantics=("arbitrary",), vmem_limit_bytes=VMEM_LIMIT),
        name="ffn_stream",
    )(n, x1, wg, wu, wd)


def _moe_kernel(te_ref, tv_ref, src_ref, n_hbm, wg_ref, wu_ref, wd_ref, y_ref, x_s, acc_s, sem, *, tme):
    i = pl.program_id(0)
    j = pl.program_id(1)
    nj = pl.num_programs(1)

    @pl.when(tv_ref[i] > 0)
    def _():
        @pl.when(j == 0)
        def _():
            def issue(r, c):
                tok = src_ref[0, 0, r]
                pltpu.make_async_copy(n_hbm.at[pl.ds(tok, 1), :], x_s.at[pl.ds(r, 1), :], sem).start()
                return c
            lax.fori_loop(0, tme, issue, 0)
            pltpu.make_async_copy(n_hbm.at[pl.ds(0, tme), :], x_s, sem).wait()
            acc_s[...] = jnp.zeros_like(acc_s)

        x = x_s[...].astype(BF16)
        h = _silu(_dot(x, wg_ref[0])) * _dot(x, wu_ref[0])
        acc_s[...] += _dot(h.astype(BF16), wd_ref[0])

        @pl.when(j == nj - 1)
        def _():
            y_ref[...] = acc_s[...]

    @pl.when((tv_ref[i] == 0) & (j == nj - 1))
    def _():
        y_ref[...] = jnp.zeros_like(y_ref)


def _moe_call(n_all, tile_expert, tile_valid, src_tok3, wg, wu, wd, tme, tf):
    n_tiles = tile_expert.shape[0]
    d_ff = wg.shape[2]
    nj = d_ff // tf

    def jsel(i, j, te, tv):
        return jnp.where(tv[i] > 0, j, nj - 1)

    grid_spec = pltpu.PrefetchScalarGridSpec(
        num_scalar_prefetch=2,
        grid=(n_tiles, nj),
        in_specs=[
            pl.BlockSpec((1, 1, tme), lambda i, j, te, tv: (i, 0, 0), memory_space=pltpu.SMEM),
            pl.BlockSpec(memory_space=pl.ANY),
            pl.BlockSpec((1, D_MODEL, tf), lambda i, j, te, tv: (te[i], 0, jsel(i, j, te, tv))),
            pl.BlockSpec((1, D_MODEL, tf), lambda i, j, te, tv: (te[i], 0, jsel(i, j, te, tv))),
            pl.BlockSpec((1, tf, D_MODEL), lambda i, j, te, tv: (te[i], jsel(i, j, te, tv), 0)),
        ],
        out_specs=pl.BlockSpec((tme, D_MODEL), lambda i, j, te, tv: (i, 0)),
        scratch_shapes=[pltpu.VMEM((tme, D_MODEL), F32), pltpu.VMEM((tme, D_MODEL), F32), pltpu.SemaphoreType.DMA],
    )
    return pl.pallas_call(
        functools.partial(_moe_kernel, tme=tme),
        grid_spec=grid_spec,
        out_shape=jax.ShapeDtypeStruct((n_tiles * tme, D_MODEL), F32),
        compiler_params=pltpu.CompilerParams(dimension_semantics=("arbitrary", "arbitrary"), vmem_limit_bytes=VMEM_LIMIT),
        name="moe_experts",
    )(tile_expert, tile_valid, src_tok3, n_all, wg, wu, wd)


def _route_plan(route_idx, tme, n_tiles):
    n_tok = route_idx.shape[0]
    e_flat = route_idx.reshape(-1)
    onehot = (e_flat[:, None] == jnp.arange(N_EXPERTS, dtype=I32)[None, :]).astype(I32)
    csum = jnp.cumsum(onehot, axis=0)
    rank = jnp.sum(csum * onehot, axis=1) - 1
    cnt = csum[-1]
    tiles_e = (cnt + tme - 1) // tme
    tile_end = jnp.cumsum(tiles_e)
    row_off = (tile_end - tiles_e) * tme
    dst = jnp.sum(onehot * row_off[None, :], axis=1) + rank
    tile_id = jnp.arange(n_tiles, dtype=I32)
    tile_expert = jnp.minimum(jnp.sum((tile_id[:, None] >= tile_end[None, :]).astype(I32), axis=1), N_EXPERTS - 1)
    tile_valid = (tile_id < tile_end[-1]).astype(I32)
    tok_of = jnp.arange(2 * n_tok, dtype=I32) // 2
    src_tok = jnp.zeros((n_tiles * tme,), I32).at[dst].set(tok_of)
    return dst.reshape(n_tok, 2), src_tok.reshape(n_tiles, 1, tme), tile_expert.astype(I32), tile_valid


def _moe_combine_kernel(pos_ref, y_hbm, x1_ref, rw_ref, g_ref, o_ref, y_s, sem, *, tm, final_norm):
    def issue(r, c):
        p0 = pos_ref[0, 0, 2 * r]
        p1 = pos_ref[0, 0, 2 * r + 1]
        pltpu.make_async_copy(y_hbm.at[pl.ds(p0, 1), :], y_s.at[0, pl.ds(r, 1), :], sem).start()
        pltpu.make_async_copy(y_hbm.at[pl.ds(p1, 1), :], y_s.at[1, pl.ds(r, 1), :], sem).start()
        return c
    lax.fori_loop(0, tm, issue, 0)
    for slot in range(2):
        pltpu.make_async_copy(y_hbm.at[pl.ds(0, tm), :], y_s.at[slot], sem).wait()
    rw = rw_ref[...]
    out = x1_ref[...] + rw[:, 0:1] * y_s[0] + rw[:, 1:2] * y_s[1]
    if final_norm:
        out = _rmsnorm(out, g_ref[...])
    o_ref[...] = out


def _moe_combine_call(pos3, y_sorted, x1, rw, g_final, tm, final_norm):
    m = x1.shape[0]
    row = lambda i: (i, 0)
    return pl.pallas_call(
        functools.partial(_moe_combine_kernel, tm=tm, final_norm=final_norm),
        grid=(m // tm,),
        in_specs=[
            pl.BlockSpec((1, 1, 2 * tm), lambda i: (i, 0, 0), memory_space=pltpu.SMEM),
            pl.BlockSpec(memory_space=pl.ANY),
            pl.BlockSpec((tm, D_MODEL), row),
            pl.BlockSpec((tm, LANES), row),
            pl.BlockSpec((1, D_MODEL), lambda i: (0, 0)),
        ],
        out_specs=pl.BlockSpec((tm, D_MODEL), row),
        out_shape=jax.ShapeDtypeStruct((m, D_MODEL), F32),
        scratch_shapes=[pltpu.VMEM((2, tm, D_MODEL), F32), pltpu.SemaphoreType.DMA],
        compiler_params=pltpu.CompilerParams(dimension_semantics=("arbitrary",), vmem_limit_bytes=VMEM_LIMIT),
        name="moe_combine",
    )(pos3, y_sorted, x1, rw, g_final)


def _final_norm_kernel(x_ref, g_ref, o_ref):
    o_ref[...] = _rmsnorm(x_ref[...], g_ref[...])


def _final_norm_call(x, g, tm):
    m = x.shape[0]
    row = lambda i: (i, 0)
    return pl.pallas_call(
        _final_norm_kernel,
        grid=(m // tm,),
        in_specs=[pl.BlockSpec((tm, D_MODEL), row), pl.BlockSpec((1, D_MODEL), lambda i: (0, 0))],
        out_specs=pl.BlockSpec((tm, D_MODEL), row),
        out_shape=jax.ShapeDtypeStruct((m, D_MODEL), F32),
        compiler_params=pltpu.CompilerParams(dimension_semantics=("arbitrary",)),
        name="final_norm",
    )(x, g)


def _rope_tables(pos):
    half = HEAD_DIM // 2
    inv_freq = np.power(np.float64(ROPE_THETA), -np.arange(half, dtype=np.float64) * (2.0 / HEAD_DIM))
    ang = np.asarray(pos, np.float64)[:, None] * inv_freq[None, :]
    c, s = np.cos(ang), np.sin(ang)
    cos_t = np.concatenate([c, c, c, c], axis=1).astype(np.float32)
    sin_t = np.concatenate([-s, s, -s, s], axis=1).astype(np.float32)
    return jnp.asarray(cos_t), jnp.asarray(sin_t)


def _pack_w_in(w):
    gates0 = QKV_W + GROUP_W
    pad = jnp.zeros((w.shape[0], LANES - 2 * N_HEADS), w.dtype)
    return jnp.concatenate([w[:, :gates0], w[:, gates0 + 2 * N_HEADS:], w[:, gates0:gates0 + 2 * N_HEADS], pad],
                           axis=1)


def _lane_row(vals, offset):
    return jnp.zeros((1, LANES), F32).at[0, offset:offset + vals.shape[0]].set(vals.astype(F32))


def _pad_rows(x, rows, front=0):
    return jnp.pad(x, ((0, 0), (front, rows - front - x.shape[1]), (0, 0)))


def kernel(x_prompt, x_sample, state_conv, state_delta, cache_win_k, cache_win_v, norm_mix, w_in, conv_w, a_log,
           dt_bias, gdn_norm, w_out, norm_ffn, ffn_gate, ffn_up, ffn_down, router, moe_gate, moe_up, moe_down,
           final_norm):
    bp, seq, _ = x_prompt.shape
    bs, t_len, _ = x_sample.shape
    depth = w_in.shape[0]
    n_p, n_s = bp * seq, bs * t_len
    n_all = n_p + n_s
    tm_p, tm_s = 512, n_s
    gdn_rows = 256
    n_buf = cache_win_k.shape[2]
    n_keep = min(WINDOW_MAX, seq)

    cos_p, sin_p = _rope_tables(np.arange(seq))
    cos_s, sin_s = _rope_tables(PAST_LEN + (np.arange(n_s) % t_len))
    xp = x_prompt.reshape(n_p, D_MODEL)
    xs = x_sample.reshape(n_s, D_MODEL)
    g_final = final_norm.reshape(1, D_MODEL)

    conv_p, delta_p, wk_p, wv_p = [], [], [], []
    conv_s, delta_s = [], []
    new_k_all = new_v_all = None
    for l in range(depth):
        w_packed_f = _pack_w_in(w_in[l].astype(F32))
        w_packed = w_packed_f.astype(BF16)
        w_out_f = w_out[l].astype(F32)
        g_mix = norm_mix[l].reshape(1, D_MODEL)
        alog_row = _lane_row(a_log[l], N_HEADS)
        dtb_row = _lane_row(dt_bias[l], N_HEADS)
        gn_row = jnp.tile(gdn_norm[l].astype(F32), N_HEADS).reshape(1, GROUP_W)
        w_out_b = w_out[l].astype(BF16)
        g_ffn = norm_ffn[l].reshape(1, D_MODEL)
        is_moe = l % 2 == 1
        idx = l // 2

        dils = tuple(d for _, d in DILATIONS)
        qkv, z, gt, k, v, *qkv_d = _inproj_call(xp, g_mix, w_packed, cos_p, sin_p, tm_p, dils=dils)
        oa, s_bd = _gdn_prompt_call(qkv, z, gt, conv_w[l], alog_row, dtb_row, gn_row, bp, seq, gdn_rows)
        branches = [_swa_call(*qkv_d[3 * i:3 * i + 3], dil, bp, seq) for i, dil in enumerate(dils)]
        conv_p.append(qkv.reshape(bp, seq, QKV_W)[:, seq - (CONV_TAPS - 1):])
        delta_p.append(_state_from_block_diag(s_bd))
        wk_p.append(k.reshape(bp, seq, N_HEADS, HEAD_DIM)[:, seq - n_keep:])
        wv_p.append(v.reshape(bp, seq, N_HEADS, HEAD_DIM)[:, seq - n_keep:])

        qkv_s, z_s, gt_s, q_s, k_s, v_s = _inproj_call(xs, g_mix, w_packed_f, cos_s, sin_s, tm_s, precise=True)
        prev_pad = _pad_rows(state_conv[l].astype(F32), 8, front=8 - (CONV_TAPS - 1))
        oa_s_pad, s_bd_s = _gdn_sample_call(
            _pad_rows(qkv_s.reshape(bs, t_len, QKV_W), CHUNK), prev_pad,
            _pad_rows(z_s.reshape(bs, t_len, GROUP_W), CHUNK), _pad_rows(gt_s.reshape(bs, t_len, LANES), CHUNK),
            conv_w[l], alog_row, dtb_row, gn_row, _state_to_block_diag(state_delta[l]), t_len, 4)
        oa_s = oa_s_pad[:, :t_len].reshape(n_s, GROUP_W)
        heads = (bs, t_len, N_HEADS, HEAD_DIM)
        new_k_all, new_v_all, o_s3, l_s3 = _swa_sample_call(
            q_s.reshape(heads), k_s.reshape(heads), v_s.reshape(heads), cache_win_k.astype(F32),
            cache_win_v.astype(F32), l, new_k_all, new_v_all)
        conv_all = jnp.concatenate([state_conv[l].astype(F32), qkv_s.reshape(bs, t_len, QKV_W)], axis=1)
        conv_s.append(conv_all[:, conv_all.shape[1] - (CONV_TAPS - 1):])
        delta_s.append(_state_from_block_diag(s_bd_s))
        nd = len(DILATIONS)
        o_br_s = [o_s3[:, i * t_len:(i + 1) * t_len].reshape(n_s, GROUP_W) for i in range(nd)]
        l_br_s = [jnp.pad(l_s3[:, i * t_len:(i + 1) * t_len, :, 0].reshape(n_s, N_HEADS),
                          ((0, 0), (0, LANES - N_HEADS))) for i in range(nd)]

        o_br_p = [b[0] for b in branches]
        l_br_p = [b[1] for b in branches]
        if not is_moe:
            x1p, np_ = _outproj_call(oa.reshape(n_p, GROUP_W), o_br_p, l_br_p, xp, w_out_b, g_ffn, None, tm_p, BF16,
                                     dils=dils)
            x1s, ns_ = _outproj_call(oa_s, o_br_s, l_br_s, xs, w_out_f, g_ffn, None, tm_s, F32, precise=True)
            wg, wu, wd = ffn_gate[idx].astype(BF16), ffn_up[idx].astype(BF16), ffn_down[idx].astype(BF16)
            xp = _ffn_call(np_, x1p, wg, wu, wd, tm_p)
            xs = _ffn_stream_call(ns_, x1s, ffn_gate[idx].astype(F32), ffn_up[idx].astype(F32),
                                  ffn_down[idx].astype(F32), 256, precise=True)
            if l == depth - 1:
                xp = _final_norm_call(xp, g_final, tm_p)
                xs = _final_norm_call(xs, g_final, tm_s)
        else:
            router_pad = jnp.pad(router[idx].astype(F32), ((0, 0), (0, LANES - N_EXPERTS)))
            x1p, n_tok_p, ri_p, rw_p = _outproj_call(
                oa.reshape(n_p, GROUP_W), o_br_p, l_br_p, xp, w_out_b, g_ffn, router_pad, tm_p, F32, dils=dils)
            x1s, n_tok_s, ri_s, rw_s = _outproj_call(oa_s, o_br_s, l_br_s, xs, w_out_f, g_ffn, router_pad, tm_s, F32,
                                                     precise=True)
            n_tok = jnp.concatenate([n_tok_p, n_tok_s], axis=0)
            tme, tf = 512, 512
            n_tiles = (2 * n_all + N_EXPERTS * (tme - 1)) // tme
            ridx = jnp.concatenate([ri_p[:, :2], ri_s[:, :2]], axis=0)
            dst, src_tok3, tile_expert, tile_valid = _route_plan(ridx, tme, n_tiles)
            y_sorted = _moe_call(n_tok, tile_expert, tile_valid, src_tok3, moe_gate[idx].astype(BF16),
                                 moe_up[idx].astype(BF16), moe_down[idx].astype(BF16), tme, tf)
            last = l == depth - 1
            tmc_p, tmc_s = 256, n_s
            xp = _moe_combine_call(dst[:n_p].reshape(n_p // tmc_p, 1, 2 * tmc_p), y_sorted, x1p, rw_p, g_final,
                                   tmc_p, last)
            xs = _moe_combine_call(dst[n_p:].reshape(n_s // tmc_s, 1, 2 * tmc_s), y_sorted, x1s, rw_s, g_final,
                                   tmc_s, last)

    y_prompt = xp.reshape(bp, seq, D_MODEL)
    y_sample = xs.reshape(bs, t_len, D_MODEL)
    return (y_prompt, y_sample, jnp.stack(conv_p), jnp.stack(delta_p), jnp.stack(wk_p), jnp.stack(wv_p),
            jnp.stack(conv_s), jnp.stack(delta_s), new_k_all, new_v_all)
```

```python
import functools

import numpy as np
import jax
import jax.numpy as jnp
from jax import lax
from jax.experimental import pallas as pl
from jax.experimental.pallas import tpu as pltpu

F32 = jnp.float32
BF16 = jnp.bfloat16
I32 = jnp.int32

EPS = 1e-6
NEG_BIG = -1e30
D_MODEL = 1024
HEAD_DIM = 64
N_HEADS = 8
GROUP_W = N_HEADS * HEAD_DIM
QKV_W = 3 * GROUP_W
CONV_TAPS = 4
CHUNK = 64
HEADS_PER_PACK = 4
PACK_W = HEADS_PER_PACK * HEAD_DIM
DILATIONS = ((128, 1), (512, 4), (2048, 16))
BAND = 128
WINDOW_MAX = 2048
ROPE_THETA = 10000.0
PAST_LEN = 16384
N_EXPERTS = 8
PROJ_PACKED_W = QKV_W + 4 * GROUP_W + 128
LANES = 128
VMEM_LIMIT = 56 * 1024 * 1024


def _dot(a, b):
    return jnp.dot(a, b, preferred_element_type=F32)


def _dot_nt(a, b):
    return lax.dot_general(a, b, (((1,), (1,)), ((), ())), preferred_element_type=F32)


def _mm(a, b, precise=False):
    if precise:
        return jnp.dot(a.astype(F32), b.astype(F32), preferred_element_type=F32, precision=lax.Precision.HIGHEST)
    return jnp.dot(a.astype(BF16), b.astype(BF16), preferred_element_type=F32)


def _mm_nt(a, b, precise=False):
    if precise:
        return lax.dot_general(a.astype(F32), b.astype(F32), (((1,), (1,)), ((), ())), preferred_element_type=F32,
                               precision=lax.Precision.HIGHEST)
    return _dot_nt(a.astype(BF16), b.astype(BF16))


def _split2(x):
    hi = x.astype(BF16)
    lo = (x - hi.astype(F32)).astype(BF16)
    return hi, lo


def _split3(x):
    hi = x.astype(BF16)
    r = x - hi.astype(F32)
    mid = r.astype(BF16)
    lo = (r - mid.astype(F32)).astype(BF16)
    return hi, mid, lo


def _dot_x3(a, b):
    ah, al = _split2(a)
    bh, bl = _split2(b)
    m = a.shape[0]
    r = _dot(jnp.concatenate([ah, al], axis=0), bh)
    return r[:m] + r[m:] + _dot(ah, bl)


def _dot_exact_rhs01(x, e):
    return sum(_dot(p, e) for p in _split3(x))


def _sigmoid(x):
    return 1.0 / (1.0 + jnp.exp(-x))


def _silu(x):
    return x * _sigmoid(x)


def _softplus(x):
    return jnp.maximum(x, 0.0) + jnp.log1p(jnp.exp(-jnp.abs(x)))


def _rmsnorm(x, g):
    ms = jnp.mean(x * x, axis=-1, keepdims=True)
    return x * lax.rsqrt(ms + EPS) * g


def _head_ones(width):
    r = lax.broadcasted_iota(I32, (width, width), 0) // HEAD_DIM
    c = lax.broadcasted_iota(I32, (width, width), 1) // HEAD_DIM
    return jnp.where(r == c, 1.0, 0.0).astype(BF16)


def _head_sum(x, ones_bd):
    hi, lo = _split2(x)
    return _dot(hi, ones_bd) + _dot(lo, ones_bd)


def _rope(x, cos, sin):
    n = x.shape[1]
    reps = n // LANES
    c = jnp.concatenate([cos] * reps, axis=1)
    s = jnp.concatenate([sin] * reps, axis=1)
    lane = lax.broadcasted_iota(I32, x.shape, 1)
    first_half = (lane & (HEAD_DIM - 1)) < HEAD_DIM // 2
    partner = jnp.where(first_half, pltpu.roll(x, n - HEAD_DIM // 2, 1), pltpu.roll(x, HEAD_DIM // 2, 1))
    return x * c + partner * s


def _residue_perm(tm, dil):
    i = np.arange(tm)
    src = (i % (tm // dil)) * dil + i // (tm // dil)
    p = np.zeros((tm, tm), np.float32)
    p[i, src] = 1.0
    return p


def _inproj_kernel(x_ref, g_ref, w_ref, cos_ref, sin_ref, *rest, precise, dils):
    y = _rmsnorm(x_ref[...], g_ref[...])
    if not precise:
        y = y.astype(BF16)

    def mm(lo, hi):
        return _mm(y, w_ref[:, lo:hi], precise)

    cos = cos_ref[...]
    sin = sin_ref[...]
    n_perm = 0 if dils is None else sum(1 for d in dils if d > 1)
    perm_refs, outs = rest[:n_perm], rest[n_perm:]
    o = 0
    outs[0][...] = mm(o, o + QKV_W)
    o += QKV_W
    outs[1][...] = mm(o, o + GROUP_W)
    o += GROUP_W
    q = _rope(mm(o, o + GROUP_W), cos, sin)
    o += GROUP_W
    k = _rope(mm(o, o + GROUP_W), cos, sin)
    o += GROUP_W
    v = mm(o, o + GROUP_W)
    o += GROUP_W
    outs[2][...] = mm(o, o + LANES)
    if dils is None:
        outs[3][...] = q
        outs[4][...] = k
        outs[5][...] = v
        return
    outs[3][...] = k
    outs[4][...] = v
    tm = q.shape[0]
    qkv_b = [t.astype(BF16) for t in (q, k, v)]
    nxt, pi = 5, 0
    for d in dils:
        if d == 1:
            for t in qkv_b:
                outs[nxt][...] = t
                nxt += 1
            continue
        sub = tm // d
        for t in qkv_b:
            tp = _dot(perm_refs[pi][...], t).astype(BF16)
            for r in range(d):
                outs[nxt][:, r * GROUP_W:(r + 1) * GROUP_W] = tp[r * sub:(r + 1) * sub, :]
            nxt += 1
        pi += 1


def _inproj_call(x, g, w_packed, cos_tab, sin_tab, tm, precise=False, dils=None):
    m = x.shape[0]
    n_tab = cos_tab.shape[0] // tm
    row = lambda i: (i, 0)
    const = lambda i: (0, 0)
    tab = lambda i: (i % n_tab, 0)
    in_specs = [
        pl.BlockSpec((tm, D_MODEL), row),
        pl.BlockSpec((1, D_MODEL), const),
        pl.BlockSpec((D_MODEL, PROJ_PACKED_W), const),
        pl.BlockSpec((tm, LANES), tab),
        pl.BlockSpec((tm, LANES), tab),
    ]
    args = [x, g, w_packed, cos_tab, sin_tab]
    outs = [(m, QKV_W, tm, F32), (m, GROUP_W, tm, F32), (m, LANES, tm, F32)]
    if dils is None:
        outs += [(m, GROUP_W, tm, F32)] * 3
    else:
        outs += [(m, GROUP_W, tm, F32)] * 2
        for d in dils:
            outs += [(m // d, d * GROUP_W, tm // d, BF16)] * 3
            if d > 1:
                in_specs.append(pl.BlockSpec((tm, tm), const))
                args.append(jnp.asarray(_residue_perm(tm, d), BF16))
    return pl.pallas_call(
        functools.partial(_inproj_kernel, precise=precise, dils=dils),
        grid=(m // tm,),
        in_specs=in_specs,
        out_specs=[pl.BlockSpec((rows, w), row) for _, w, rows, _ in outs],
        out_shape=[jax.ShapeDtypeStruct((n, w), dt) for n, w, _, dt in outs],
        compiler_params=pltpu.CompilerParams(dimension_semantics=("arbitrary",), vmem_limit_bytes=VMEM_LIMIT),
        name="inproj",
    )(*args)


def _pack_masks():
    lane = lax.broadcasted_iota(I32, (1, PACK_W), 1)
    return [lane // HEAD_DIM == h for h in range(HEADS_PER_PACK)]


def _block_diag(x, head_masks):
    return jnp.concatenate([jnp.where(m, x, 0.0) for m in head_masks], axis=0)


def _gdn_chunk_consts():
    i = lax.broadcasted_iota(I32, (CHUNK, PACK_W), 0)
    j = lax.broadcasted_iota(I32, (CHUNK, PACK_W), 1) & (HEAD_DIM - 1)
    r = lax.broadcasted_iota(I32, (CHUNK, CHUNK), 0)
    c = lax.broadcasted_iota(I32, (CHUNK, CHUNK), 1)
    br = lax.broadcasted_iota(I32, (PACK_W, PACK_W), 0) // HEAD_DIM
    bc = lax.broadcasted_iota(I32, (PACK_W, PACK_W), 1) // HEAD_DIM
    return dict(
        head_masks=_pack_masks(),
        lower=i >= j,
        strict=i > j,
        upper_f=jnp.where(i <= j, 1.0, 0.0),
        eye=jnp.where(i == j, 1.0, 0.0),
        tri=jnp.where(r >= c, 1.0, 0.0).astype(BF16),
        block=br == bc,
    )


def _gdn_chunk(qn, kn, v, bw, gw, state, cst, precise):
    hm = cst["head_masks"]
    gcum = sum(_dot(cst["tri"], p) for p in _split3(gw))
    grow = jnp.sum(gw * cst["upper_f"], axis=0, keepdims=True)
    diff = gcum - grow
    decay = jnp.where(cst["lower"], jnp.exp(jnp.where(cst["lower"], diff, 0.0)), 0.0)
    kq = _mm_nt(jnp.concatenate([kn, qn], axis=0), _block_diag(kn, hm), precise)
    kk, qk = kq[:CHUNK], kq[CHUNK:]
    nmat = bw * kk * jnp.where(cst["strict"], decay, 0.0)
    p = -nmat
    y = cst["eye"] + p
    q = _dot_x3(p, _block_diag(p, hm))
    n_sq = CHUNK.bit_length() - 1
    for s in range(1, n_sq):
        q_bd = _block_diag(q, hm)
        if s < n_sq - 1:
            both = _dot_x3(jnp.concatenate([y, q], axis=0), q_bd)
            y = y + both[:CHUNK]
            q = both[CHUNK:]
        else:
            y = y + _dot_x3(y, q_bd)
    egc = jnp.exp(gcum)
    rhs = jnp.concatenate([_block_diag(bw * v, hm), _block_diag(bw * egc * kn, hm)], axis=1)
    sol = _mm(y, rhs, precise)
    u, wk = sol[:, :PACK_W], sol[:, PACK_W:]
    aqk = qk * decay
    qd = qn * egc
    glast = gcum[CHUNK - 1:CHUNK, :]
    kd = kn * jnp.exp(glast - gcum)
    m1 = _mm(jnp.concatenate([wk, qd], axis=0), state, precise)
    w = u - m1[:CHUNK]
    o = m1[CHUNK:] + _mm(aqk, _block_diag(w, hm), precise)
    inc = _mm(jnp.transpose(kd), w, precise)
    new_state = state * jnp.exp(glast) + jnp.where(cst["block"], inc, 0.0)
    return o, new_state


def _bmm(a, b, precise=False):
    dn = (((2,), (1,)), ((0,), (0,)))
    if precise:
        return lax.dot_general(a.astype(F32), b.astype(F32), dn, preferred_element_type=F32,
                               precision=lax.Precision.HIGHEST)
    return lax.dot_general(a.astype(BF16), b.astype(BF16), dn, preferred_element_type=F32)


def _bmm_nt(a, b, precise=False):
    dn = (((2,), (2,)), ((0,), (0,)))
    if precise:
        return lax.dot_general(a.astype(F32), b.astype(F32), dn, preferred_element_type=F32,
                               precision=lax.Precision.HIGHEST)
    return lax.dot_general(a.astype(BF16), b.astype(BF16), dn, preferred_element_type=F32)


def _block_diag_b(x, head_masks):
    return jnp.concatenate([jnp.where(m, x, 0.0) for m in head_masks], axis=1)


def _gdn_local(qn, kn, v, bw, gw, cst, precise):
    hm = cst["head_masks"]
    nb = qn.shape[0]
    tri = jnp.broadcast_to(cst["tri"], (nb, CHUNK, CHUNK))
    gcum = sum(_bmm(tri, p) for p in _split3(gw))
    grow = jnp.sum(gw * cst["upper_f"], axis=1, keepdims=True)
    diff = gcum - grow
    decay = jnp.where(cst["lower"], jnp.exp(jnp.where(cst["lower"], diff, 0.0)), 0.0)
    kq = _bmm_nt(jnp.concatenate([kn, qn], axis=1), _block_diag_b(kn, hm), precise)
    kk, qk = kq[:, :CHUNK], kq[:, CHUNK:]
    nmat = bw * kk * jnp.where(cst["strict"], decay, 0.0)
    p = -nmat
    y = cst["eye"] + p
    q = _bmm(p, _block_diag_b(p, hm), precise)
    n_sq = CHUNK.bit_length() - 1
    for s in range(1, n_sq):
        q_bd = _block_diag_b(q, hm)
        if s < n_sq - 1:
            both = _bmm(jnp.concatenate([y, q], axis=1), q_bd, precise)
            y = y + both[:, :CHUNK]
            q = both[:, CHUNK:]
        else:
            y = y + _bmm(y, q_bd, precise)
    egc = jnp.exp(gcum)
    rhs = jnp.concatenate([_block_diag_b(bw * v, hm), _block_diag_b(bw * egc * kn, hm)], axis=2)
    sol = _bmm(y, rhs, precise)
    glast = gcum[:, CHUNK - 1:CHUNK, :]
    return (sol[:, :, :PACK_W], sol[:, :, PACK_W:], qk * decay, qn * egc, kn * jnp.exp(glast - gcum),
            jnp.exp(glast))


def _gdn_scan_step(u, wk, aqk, qd, kd, gl, state, cst, precise):
    hm = cst["head_masks"]
    m1 = _bmm(jnp.concatenate([wk, qd], axis=1), state, precise)
    w = u - m1[:, :CHUNK]
    o = m1[:, CHUNK:] + _bmm(aqk, _block_diag_b(w, hm), precise)
    inc = _bmm(jnp.swapaxes(kd, 1, 2), w, precise)
    return o, state * gl + jnp.where(cst["block"], inc, 0.0)


def _gdn_kernel(*refs, nb, rows, carry, n_valid, precise):
    if carry:
        (qkv_ref, prev_ref, z_ref, gt_ref, cw_ref, al_ref, db_ref, gn_ref, o_ref, sout_ref, xp_s, st_s) = refs
        s0_ref = None
    else:
        (qkv_ref, prev_ref, z_ref, gt_ref, cw_ref, al_ref, db_ref, gn_ref, s0_ref, o_ref, sout_ref, xp_s, st_s) = refs
    step = pl.program_id(0)
    ones_bd = _head_ones(GROUP_W)
    lane = lax.broadcasted_iota(I32, (1, LANES), 1)
    er = lax.broadcasted_iota(I32, (LANES, 2 * GROUP_W), 0)
    ec = lax.broadcasted_iota(I32, (LANES, 2 * GROUP_W), 1)
    expand = jnp.where(((ec < GROUP_W) & (er == ec // HEAD_DIM))
                       | ((ec >= GROUP_W) & (er == N_HEADS + (ec - GROUP_W) // HEAD_DIM)), 1.0, 0.0).astype(BF16)
    cw = cw_ref[...]

    if carry:
        @pl.when(step == 0)
        def _():
            st_s[...] = jnp.zeros_like(st_s)
    else:
        st_s[...] = s0_ref[...].reshape(st_s.shape)

    per_seq = []
    for b in range(nb):
        prev = prev_ref[b]
        if carry:
            prev = jnp.where(step == 0, 0.0, prev)
        xp_s[b, 0:8, :] = prev
        xp_s[b, 8:8 + rows, :] = qkv_ref[b]
        acc = cw[CONV_TAPS - 1:CONV_TAPS, :] * xp_s[b, 8:8 + rows, :]
        for t in range(1, CONV_TAPS):
            acc = acc + cw[CONV_TAPS - 1 - t:CONV_TAPS - t, :] * xp_s[b, 8 - t:8 - t + rows, :]
        cq = _silu(acc)
        q = cq[:, :GROUP_W]
        k = cq[:, GROUP_W:2 * GROUP_W]
        v = cq[:, 2 * GROUP_W:]
        qn = q * lax.rsqrt(_head_sum(q * q, ones_bd) + EPS) * (HEAD_DIM ** -0.5)
        kn = k * lax.rsqrt(_head_sum(k * k, ones_bd) + EPS)
        t_raw = gt_ref[b]
        beta = _sigmoid(t_raw)
        gdec = -jnp.exp(al_ref[...]) * _softplus(t_raw + db_ref[...])
        wide = _dot_exact_rhs01(jnp.where(lane < N_HEADS, beta, gdec), expand)
        bw = wide[:, :GROUP_W]
        gw = wide[:, GROUP_W:]
        if n_valid < rows:
            live = lax.broadcasted_iota(I32, (rows, 1), 0) < n_valid
            bw = jnp.where(live, bw, 0.0)
            gw = jnp.where(live, gw, 0.0)
            qn = jnp.where(live, qn, 0.0)
            kn = jnp.where(live, kn, 0.0)
            v = jnp.where(live, v, 0.0)
        per_seq.append((qn, kn, v, bw, gw))

    cst = _gdn_chunk_consts()
    n_chunks = rows // CHUNK
    n_pack = N_HEADS // HEADS_PER_PACK

    def instances(idx):
        return jnp.stack([per_seq[b][idx][c * CHUNK:(c + 1) * CHUNK, g * PACK_W:(g + 1) * PACK_W]
                          for c in range(n_chunks) for b in range(nb) for g in range(n_pack)], axis=0)

    u, wk, aqk, qd, kd, gl = _gdn_local(*(instances(i) for i in range(5)), cst, precise)
    n_chain = nb * n_pack
    state = st_s[...]
    o_chunks = []
    for c in range(n_chunks):
        sl = slice(c * n_chain, (c + 1) * n_chain)
        o_c, state = _gdn_scan_step(u[sl], wk[sl], aqk[sl], qd[sl], kd[sl], gl[sl], state, cst, precise)
        o_chunks.append(o_c)
    st_s[...] = state

    for b in range(nb):
        o = jnp.concatenate(
            [jnp.concatenate([o_chunks[c][b * n_pack + g] for g in range(n_pack)], axis=1) for c in range(n_chunks)],
            axis=0)
        ms = _head_sum(o * o, ones_bd) * (1.0 / HEAD_DIM)
        o_ref[b] = (o * lax.rsqrt(ms + EPS) * gn_ref[...] * _silu(z_ref[b])).astype(o_ref.dtype)

    if carry:
        @pl.when(step == pl.num_programs(0) - 1)
        def _():
            sout_ref[...] = state.reshape(sout_ref.shape)
    else:
        sout_ref[...] = state.reshape(sout_ref.shape)


def _gdn_scratch(nb, rows):
    return [pltpu.VMEM((nb, rows + 8, QKV_W), F32),
            pltpu.VMEM((nb * (N_HEADS // HEADS_PER_PACK), PACK_W, PACK_W), F32)]


def _gdn_prompt_call(qkv, z, gt, conv_w, alog_row, dtb_row, gn_row, batch, seq, rows):
    qkv3 = qkv.reshape(batch, seq, QKV_W)
    z3 = z.reshape(batch, seq, GROUP_W)
    gt3 = gt.reshape(batch, seq, LANES)
    blk = lambda i: (0, i, 0)
    const2 = lambda i: (0, 0)
    n_pack = N_HEADS // HEADS_PER_PACK
    return pl.pallas_call(
        functools.partial(_gdn_kernel, nb=batch, rows=rows, carry=True, n_valid=rows, precise=False),
        grid=(seq // rows,),
        in_specs=[
            pl.BlockSpec((batch, rows, QKV_W), blk),
            pl.BlockSpec((batch, 8, QKV_W), lambda i: (0, jnp.maximum(i * (rows // 8) - 1, 0), 0)),
            pl.BlockSpec((batch, rows, GROUP_W), blk),
            pl.BlockSpec((batch, rows, LANES), blk),
            pl.BlockSpec((CONV_TAPS, QKV_W), const2),
            pl.BlockSpec((1, LANES), const2),
            pl.BlockSpec((1, LANES), const2),
            pl.BlockSpec((1, GROUP_W), const2),
        ],
        out_specs=[
            pl.BlockSpec((batch, rows, GROUP_W), blk),
            pl.BlockSpec((batch, n_pack, PACK_W, PACK_W), lambda i: (0, 0, 0, 0)),
        ],
        out_shape=[
            jax.ShapeDtypeStruct((batch, seq, GROUP_W), BF16),
            jax.ShapeDtypeStruct((batch, n_pack, PACK_W, PACK_W), F32),
        ],
        scratch_shapes=_gdn_scratch(batch, rows),
        compiler_params=pltpu.CompilerParams(dimension_semantics=("arbitrary",), vmem_limit_bytes=VMEM_LIMIT),
        name="gdn_prompt",
    )(qkv3, qkv3, z3, gt3, conv_w, alog_row, dtb_row, gn_row)


def _gdn_sample_call(qkv_pad, prev_pad, z_pad, gt_pad, conv_w, alog_row, dtb_row, gn_row, s0_bd, n_valid, nb):
    n_seq = qkv_pad.shape[0]
    blk = lambda i: (i, 0, 0)
    blk4 = lambda i: (i, 0, 0, 0)
    const2 = lambda i: (0, 0)
    n_pack = N_HEADS // HEADS_PER_PACK
    return pl.pallas_call(
        functools.partial(_gdn_kernel, nb=nb, rows=CHUNK, carry=False, n_valid=n_valid, precise=True),
        grid=(n_seq // nb,),
        in_specs=[
            pl.BlockSpec((nb, CHUNK, QKV_W), blk),
            pl.BlockSpec((nb, 8, QKV_W), blk),
            pl.BlockSpec((nb, CHUNK, GROUP_W), blk),
            pl.BlockSpec((nb, CHUNK, LANES), blk),
            pl.BlockSpec((CONV_TAPS, QKV_W), const2),
            pl.BlockSpec((1, LANES), const2),
            pl.BlockSpec((1, LANES), const2),
            pl.BlockSpec((1, GROUP_W), const2),
            pl.BlockSpec((nb, n_pack, PACK_W, PACK_W), blk4),
        ],
        out_specs=[
            pl.BlockSpec((nb, CHUNK, GROUP_W), blk),
            pl.BlockSpec((nb, n_pack, PACK_W, PACK_W), blk4),
        ],
        out_shape=[
            jax.ShapeDtypeStruct((n_seq, CHUNK, GROUP_W), F32),
            jax.ShapeDtypeStruct((n_seq, n_pack, PACK_W, PACK_W), F32),
        ],
        scratch_shapes=_gdn_scratch(nb, CHUNK),
        compiler_params=pltpu.CompilerParams(dimension_semantics=("arbitrary",), vmem_limit_bytes=VMEM_LIMIT),
        name="gdn_sample",
    )(qkv_pad, prev_pad, z_pad, gt_pad, conv_w, alog_row, dtb_row, gn_row, s0_bd)


def _state_to_block_diag(s):
    n = s.shape[0]
    n_pack = N_HEADS // HEADS_PER_PACK
    s5 = s.reshape(n, n_pack, HEADS_PER_PACK, HEAD_DIM, HEAD_DIM).astype(F32)
    eye = jnp.eye(HEADS_PER_PACK, dtype=F32)
    return jnp.einsum("bghde,hk->bghdke", s5, eye).reshape(n, n_pack, PACK_W, PACK_W)


def _state_from_block_diag(s_bd):
    n = s_bd.shape[0]
    n_pack = N_HEADS // HEADS_PER_PACK
    s6 = s_bd.reshape(n, n_pack, HEADS_PER_PACK, HEAD_DIM, HEADS_PER_PACK, HEAD_DIM)
    diag = jnp.stack([s6[:, :, h, :, h, :] for h in range(HEADS_PER_PACK)], axis=2)
    return diag.reshape(n, N_HEADS, HEAD_DIM, HEAD_DIM)


def _swa_kernel(q_ref, kc_ref, kp_ref, vc_ref, vp_ref, o_ref, l_ref, *, tq):
    first_block = pl.program_id(2) == 0
    n_sub = tq // BAND
    lane = lax.broadcasted_iota(I32, (1, LANES), 1)
    low = lane < HEAD_DIM
    qi = lax.broadcasted_iota(I32, (BAND, 2 * BAND), 0)
    kj = lax.broadcasted_iota(I32, (BAND, 2 * BAND), 1)
    dist = qi + BAND - kj
    band = (dist >= 0) & (dist <= BAND)
    band_first = band & ((kj >= BAND) | jnp.logical_not(first_block))
    lane_l = lax.broadcasted_iota(I32, (BAND, LANES), 1)
    lse_acc = [jnp.zeros((BAND, LANES), F32) for _ in range(n_sub)]
    for p in range(GROUP_W // LANES):
        ls = slice(p * LANES, (p + 1) * LANES)
        q_pair = (q_ref[:, ls].astype(F32) * (HEAD_DIM ** -0.5)).astype(BF16)
        k_all = jnp.concatenate([kp_ref[:, ls], kc_ref[:, ls]], axis=0)
        v_all = jnp.concatenate([vp_ref[:, ls], vc_ref[:, ls]], axis=0)
        o_sub = [jnp.zeros((BAND, LANES), F32) for _ in range(n_sub)]
        for hh in range(2):
            hmask = low if hh == 0 else jnp.logical_not(low)
            k_h = jnp.where(hmask, k_all, jnp.zeros_like(k_all))
            v_h = jnp.where(hmask, v_all, jnp.zeros_like(v_all))
            head = 2 * p + hh
            for s in range(n_sub):
                sc = _dot_nt(q_pair[s * BAND:(s + 1) * BAND], k_h[s * BAND:s * BAND + 2 * BAND])
                valid = band_first if s == 0 else band
                sc = jnp.where(valid, sc, NEG_BIG)
                m = jnp.max(sc, axis=1, keepdims=True)
                pm = jnp.exp(sc - m)
                den = jnp.sum(pm, axis=1, keepdims=True)
                num = _dot(pm.astype(BF16), v_h[s * BAND:s * BAND + 2 * BAND])
                o_sub[s] = o_sub[s] + num / den
                lse_acc[s] = lse_acc[s] + jnp.where(lane_l == head, m + jnp.log(den), 0.0)
        for s in range(n_sub):
            o_ref[s * BAND:(s + 1) * BAND, ls] = o_sub[s].astype(o_ref.dtype)
    for s in range(n_sub):
        l_ref[s * BAND:(s + 1) * BAND, :] = lse_acc[s]


def _swa_call(qv, kv, vv, dil, batch, seq):
    sub_len = seq // dil
    tq = min(512, sub_len)
    nq = sub_len // tq
    rows = batch * seq // dil
    cur = lambda b, r, i: (b * nq + i, r)
    prv = lambda b, r, i: (jnp.maximum((b * nq + i) * (tq // BAND) - 1, 0), r)
    return pl.pallas_call(
        functools.partial(_swa_kernel, tq=tq),
        grid=(batch, dil, nq),
        in_specs=[
            pl.BlockSpec((tq, GROUP_W), cur),
            pl.BlockSpec((tq, GROUP_W), cur),
            pl.BlockSpec((BAND, GROUP_W), prv),
            pl.BlockSpec((tq, GROUP_W), cur),
            pl.BlockSpec((BAND, GROUP_W), prv),
        ],
        out_specs=[pl.BlockSpec((tq, GROUP_W), cur), pl.BlockSpec((tq, LANES), cur)],
        out_shape=[jax.ShapeDtypeStruct((rows, dil * GROUP_W), BF16),
                   jax.ShapeDtypeStruct((rows, dil * LANES), F32)],
        compiler_params=pltpu.CompilerParams(dimension_semantics=("arbitrary",) * 3, vmem_limit_bytes=VMEM_LIMIT),
        name=f"swa_d{dil}",
    )(qv, kv, kv, vv, vv)


T_PAD = 8


def _swa_sample_kernel(*refs, t_len, n_buf, n_alias):
    q_ref, kn_ref, vn_ref, ktail_ref, vtail_ref, kc_ref, vc_ref = refs[:7]
    nk_ref, nv_ref, o_ref, l_ref = refs[7 + n_alias:]
    t_row = lax.broadcasted_iota(I32, (T_PAD, n_buf), 0)
    delta_c = n_buf + t_row - lax.broadcasted_iota(I32, (T_PAD, n_buf), 1)
    t_row_n = lax.broadcasted_iota(I32, (T_PAD, LANES), 0)
    col_n = lax.broadcasted_iota(I32, (T_PAD, LANES), 1)
    delta_n = t_row_n - col_n
    lane = lax.broadcasted_iota(I32, (HEAD_DIM, LANES), 1)
    masks = []
    for window, dil in DILATIONS:
        masks.append(((delta_c >= 0) & ((delta_c & (dil - 1)) == 0) & (delta_c <= window),
                      (col_n < t_len) & (delta_n >= 0) & ((delta_n & (dil - 1)) == 0) & (delta_n <= window)))

    def x3(a_parts, b_parts, nt):
        (ah, al), (bh, bl) = a_parts, b_parts
        f = _dot_nt if nt else _dot
        r = f(jnp.concatenate([ah, al], axis=0), bh)
        return r[:T_PAD] + r[T_PAD:] + f(ah, bl)

    for h in range(N_HEADS):
        kt = kc_ref[h]
        vt = vc_ref[h]
        for src, tail_ref, dst in ((kt, ktail_ref, nk_ref), (vt, vtail_ref, nv_ref)):
            rolled = pltpu.roll(src, n_buf - t_len, 1)
            dst[h, :, 0:n_buf - LANES] = rolled[:, 0:n_buf - LANES]
            dst[h, :, n_buf - LANES:n_buf] = jnp.where(lane >= LANES - t_len, tail_ref[h], rolled[:, n_buf - LANES:])
        q_p = _split2(q_ref[h] * (HEAD_DIM ** -0.5))
        s_c = x3(q_p, _split2(kt), False)
        s_n = x3(q_p, _split2(kn_ref[h]), True)
        vt_p = _split2(vt)
        vn_p = _split2(vn_ref[h])
        for bi, (valid_c, valid_n) in enumerate(masks):
            sc = jnp.where(valid_c, s_c, NEG_BIG)
            sn = jnp.where(valid_n, s_n, NEG_BIG)
            m = jnp.maximum(jnp.max(sc, axis=1, keepdims=True), jnp.max(sn, axis=1, keepdims=True))
            pc = jnp.exp(sc - m)
            pn = jnp.exp(sn - m)
            den = jnp.sum(pc, axis=1, keepdims=True) + jnp.sum(pn, axis=1, keepdims=True)
            num = x3(_split2(pc), vt_p, True) + x3(_split2(pn), vn_p, False)
            o_ref[h, bi * T_PAD:(bi + 1) * T_PAD, :] = num / den
            l_ref[h, bi * T_PAD:(bi + 1) * T_PAD, :] = jnp.broadcast_to(m + jnp.log(den), (T_PAD, LANES))


def _swa_sample_call(q, k_new, v_new, cache_kt, cache_vt, layer, new_kt_all, new_vt_all):
    nbatch, t_len = q.shape[:2]
    n_buf = cache_kt.shape[-1]
    nd = len(DILATIONS)
    assert t_len <= T_PAD and n_buf % LANES == 0 and n_buf >= max(w for w, _ in DILATIONS)

    def head_major(x, rows):
        return jnp.pad(x.transpose(0, 2, 1, 3), ((0, 0), (0, 0), (0, rows - t_len), (0, 0)))

    def tail(x):
        return jnp.pad(x.transpose(0, 2, 3, 1), ((0, 0), (0, 0), (0, 0), (LANES - t_len, 0)))

    def spec(rows, width):
        return pl.BlockSpec((None, N_HEADS, rows, width), lambda b: (b, 0, 0, 0))

    cache = pl.BlockSpec((None, None, N_HEADS, HEAD_DIM, n_buf), lambda b: (layer, b, 0, 0, 0))
    in_specs = [spec(T_PAD, HEAD_DIM), spec(LANES, HEAD_DIM), spec(LANES, HEAD_DIM), spec(HEAD_DIM, LANES),
                spec(HEAD_DIM, LANES), cache, cache]
    args = [head_major(q, T_PAD), head_major(k_new, LANES), head_major(v_new, LANES), tail(k_new), tail(v_new),
            cache_kt, cache_vt]
    aliases = {}
    if new_kt_all is not None:
        in_specs += [pl.BlockSpec(memory_space=pl.ANY)] * 2
        args += [new_kt_all, new_vt_all]
        aliases = {7: 0, 8: 1}
    return pl.pallas_call(
        functools.partial(_swa_sample_kernel, t_len=t_len, n_buf=n_buf, n_alias=len(aliases)),
        grid=(nbatch,),
        in_specs=in_specs,
        out_specs=[cache, cache, spec(nd * T_PAD, HEAD_DIM), spec(nd * T_PAD, LANES)],
        out_shape=[jax.ShapeDtypeStruct(cache_kt.shape, F32), jax.ShapeDtypeStruct(cache_vt.shape, F32),
                   jax.ShapeDtypeStruct((nbatch, N_HEADS, nd * T_PAD, HEAD_DIM), F32),
                   jax.ShapeDtypeStruct((nbatch, N_HEADS, nd * T_PAD, LANES), F32)],
        input_output_aliases=aliases,
        compiler_params=pltpu.CompilerParams(dimension_semantics=("arbitrary",), vmem_limit_bytes=VMEM_LIMIT),
        name="swa_sample",
    )(*args)


def _outproj_kernel(*refs, route, n_dtype, precise, dils):
    oa_ref = refs[0]
    o_refs = refs[1:4]
    l_refs = refs[4:7]
    x_ref, w_ref, g_ref = refs[7:10]
    nxt = 10
    if route:
        r_ref = refs[nxt]
        nxt += 1
    n_perm = sum(1 for d in dils if d > 1)
    perm_refs = refs[nxt:nxt + n_perm]
    nxt += n_perm
    if route:
        x1_ref, n_ref, ri_ref, rw_ref = refs[nxt:]
    else:
        x1_ref, n_ref = refs[nxt:]

    def token_order(ref, d, pi, width, exact_f32):
        if d == 1:
            return ref[...]
        stacked = jnp.concatenate([ref[:, r * width:(r + 1) * width] for r in range(d)], axis=0)
        pt = perm_refs[pi][...]
        if exact_f32:
            return sum(_dot(pt, part) for part in _split3(stacked))
        return _dot(pt, stacked)

    o_vals, lses, pi = [], [], 0
    for d, o_ref, l_ref in zip(dils, o_refs, l_refs):
        o_vals.append(token_order(o_ref, d, pi, GROUP_W, False))
        lses.append(token_order(l_ref, d, pi, LANES, True))
        pi += 1 if d > 1 else 0
    mx = jnp.maximum(jnp.maximum(lses[0], lses[1]), lses[2])
    es = [jnp.exp(l - mx) for l in lses]
    inv = 1.0 / (es[0] + es[1] + es[2])
    er = lax.broadcasted_iota(I32, (LANES, GROUP_W), 0)
    ec = lax.broadcasted_iota(I32, (LANES, GROUP_W), 1)
    expand = jnp.where(er == ec // HEAD_DIM, 1.0, 0.0).astype(BF16)
    ob = None
    for e, o_val in zip(es, o_vals):
        hi, lo = _split2(e * inv)
        term = (_dot(hi, expand) + _dot(lo, expand)) * o_val
        ob = term if ob is None else ob + term
    mixed = _mm(oa_ref[...], w_ref[0:GROUP_W, :], precise) + _mm(ob, w_ref[GROUP_W:, :], precise)
    x1 = x_ref[...] + mixed
    x1_ref[...] = x1
    n = _rmsnorm(x1, g_ref[...])
    n_ref[...] = n.astype(n_dtype)
    if route:
        logits = _mm(n, r_ref[...], True) if precise else _dot_x3(n, r_ref[...])
        lane = lax.broadcasted_iota(I32, logits.shape, 1)
        logits = jnp.where(lane < N_EXPERTS, logits, NEG_BIG)
        m1 = jnp.max(logits, axis=1, keepdims=True)
        i1 = jnp.min(jnp.where(logits == m1, lane, LANES), axis=1, keepdims=True)
        rest = jnp.where(lane == i1, NEG_BIG, logits)
        m2 = jnp.max(rest, axis=1, keepdims=True)
        i2 = jnp.min(jnp.where(rest == m2, lane, LANES), axis=1, keepdims=True)
        e2 = jnp.exp(m2 - m1)
        w1 = 1.0 / (1.0 + e2)
        w2 = e2 * w1
        ri_ref[...] = jnp.where(lane == 0, i1, jnp.where(lane == 1, i2, 0))
        rw_ref[...] = jnp.where(lane == 0, w1, jnp.where(lane == 1, w2, 0.0))


def _outproj_call(oa, o_br, l_br, x, w_out, g, router_pad, tm, n_dtype, precise=False, dils=(1, 1, 1)):
    m = x.shape[0]
    route = router_pad is not None
    row = lambda i: (i, 0)
    const = lambda i: (0, 0)
    in_specs = ([pl.BlockSpec((tm, GROUP_W), row)]
                + [pl.BlockSpec((tm // d, d * GROUP_W), row) for d in dils]
                + [pl.BlockSpec((tm // d, d * LANES), row) for d in dils]
                + [pl.BlockSpec((tm, D_MODEL), row), pl.BlockSpec((D_MODEL, D_MODEL), const),
                   pl.BlockSpec((1, D_MODEL), const)])
    args = [oa, *o_br, *l_br, x, w_out, g]
    if route:
        in_specs.append(pl.BlockSpec((D_MODEL, LANES), const))
        args.append(router_pad)
    for d in dils:
        if d > 1:
            in_specs.append(pl.BlockSpec((tm, tm), const))
            args.append(jnp.asarray(_residue_perm(tm, d).T, BF16))
    out_specs = [pl.BlockSpec((tm, D_MODEL), row)] * 2
    out_shape = [jax.ShapeDtypeStruct((m, D_MODEL), F32), jax.ShapeDtypeStruct((m, D_MODEL), n_dtype)]
    if route:
        out_specs += [pl.BlockSpec((tm, LANES), row)] * 2
        out_shape += [jax.ShapeDtypeStruct((m, LANES), I32), jax.ShapeDtypeStruct((m, LANES), F32)]
    return pl.pallas_call(
        functools.partial(_outproj_kernel, route=route, n_dtype=n_dtype, precise=precise, dils=tuple(dils)),
        grid=(m // tm,),
        in_specs=in_specs,
        out_specs=out_specs,
        out_shape=out_shape,
        compiler_params=pltpu.CompilerParams(dimension_semantics=("arbitrary",), vmem_limit_bytes=VMEM_LIMIT),
        name="outproj",
    )(*args)


def _ffn_kernel(n_ref, x1_ref, wg_ref, wu_ref, wd_ref, o_ref, *, f_chunk, precise):
    n = n_ref[...]
    d_ff = wg_ref.shape[1]
    acc = x1_ref[...]
    for f0 in range(0, d_ff, f_chunk):
        h = _silu(_mm(n, wg_ref[:, f0:f0 + f_chunk], precise)) * _mm(n, wu_ref[:, f0:f0 + f_chunk], precise)
        acc = acc + _mm(h, wd_ref[f0:f0 + f_chunk, :], precise)
    o_ref[...] = acc


def _ffn_call(n, x1, wg, wu, wd, tm, precise=False):
    m = n.shape[0]
    d_ff = wg.shape[1]
    f_chunk = d_ff // 2 if (d_ff // 2) % LANES == 0 else d_ff
    row = lambda i: (i, 0)
    const = lambda i: (0, 0)
    return pl.pallas_call(
        functools.partial(_ffn_kernel, f_chunk=f_chunk, precise=precise),
        grid=(m // tm,),
        in_specs=[pl.BlockSpec((tm, D_MODEL), row), pl.BlockSpec((tm, D_MODEL), row),
                  pl.BlockSpec((D_MODEL, d_ff), const), pl.BlockSpec((D_MODEL, d_ff), const),
                  pl.BlockSpec((d_ff, D_MODEL), const)],
        out_specs=pl.BlockSpec((tm, D_MODEL), row),
        out_shape=jax.ShapeDtypeStruct((m, D_MODEL), F32),
        compiler_params=pltpu.CompilerParams(dimension_semantics=("arbitrary",), vmem_limit_bytes=VMEM_LIMIT),
        name="ffn_dense",
    )(n, x1, wg, wu, wd)


def _ffn_stream_kernel(n_ref, x1_ref, wg_ref, wu_ref, wd_ref, o_ref, *, precise):
    @pl.when(pl.program_id(0) == 0)
    def _():
        o_ref[...] = x1_ref[...]

    n = n_ref[...]
    h = _silu(_mm(n, wg_ref[...], precise)) * _mm(n, wu_ref[...], precise)
    o_ref[...] += _mm(h, wd_ref[...], precise)


def _ffn_stream_call(n, x1, wg, wu, wd, tf, precise):
    m = n.shape[0]
    d_ff = wg.shape[1]
    const = lambda j: (0, 0)
    return pl.pallas_call(
        functools.partial(_ffn_stream_kernel, precise=precise),
        grid=(d_ff // tf,),
        in_specs=[pl.BlockSpec((m, D_MODEL), const), pl.BlockSpec((m, D_MODEL), const),
                  pl.BlockSpec((D_MODEL, tf), lambda j: (0, j)), pl.BlockSpec((D_MODEL, tf), lambda j: (0, j)),
                  pl.BlockSpec((tf, D_MODEL), lambda j: (j, 0))],
        out_specs=pl.BlockSpec((m, D_MODEL), const),
        out_shape=jax.ShapeDtypeStruct((m, D_MODEL), F32),
        compiler_params=pltpu.CompilerParams(dimension_semantics=("arbitrary",), vmem_limit_bytes=VMEM_LIMIT),
        name="ffn_stream",
    )(n, x1, wg, wu, wd)


def _moe_kernel(te_ref, tv_ref, src_ref, nxt_ref, n_hbm, wg_ref, wu_ref, wd_ref, y_ref, x_s, acc_s, sems, *, tme, nj):
    i = pl.program_id(0)
    j = pl.program_id(1)
    n_tiles = pl.num_programs(0)
    slot = i % 2
    share = tme // nj

    def tile_wait(s):
        pltpu.make_async_copy(n_hbm.at[pl.ds(0, tme), :], x_s.at[s], sems.at[s]).wait()

    @pl.when(tv_ref[i] > 0)
    def _():
        @pl.when((i == 0) & (j == 0))
        def _():
            def issue(r, c):
                pltpu.make_async_copy(n_hbm.at[pl.ds(src_ref[0, 0, r], 1), :], x_s.at[0, pl.ds(r, 1), :],
                                      sems.at[0]).start()
                return c
            lax.fori_loop(0, tme, issue, 0, unroll=8)

        @pl.when(j == 0)
        def _():
            tile_wait(slot)
            acc_s[...] = jnp.zeros_like(acc_s)

        x = x_s[slot].astype(BF16)
        h = _silu(_mm(x, wg_ref[0])) * _mm(x, wu_ref[0])
        acc_s[...] += _mm(h, wd_ref[0])
        for k in range(share):
            r = j * share + k
            pltpu.make_async_copy(n_hbm.at[pl.ds(nxt_ref[0, 0, r], 1), :], x_s.at[1 - slot, pl.ds(r, 1), :],
                                  sems.at[1 - slot]).start()

        @pl.when(j == nj - 1)
        def _():
            y_ref[...] = acc_s[...]

            @pl.when(tv_ref[jnp.minimum(i + 1, n_tiles - 1)] * (i + 1 < n_tiles).astype(I32) == 0)
            def _():
                tile_wait(1 - slot)

    @pl.when((tv_ref[i] == 0) & (j == nj - 1))
    def _():
        y_ref[...] = jnp.zeros_like(y_ref)


def _moe_call(n_all, tile_expert, tile_valid, src_tok3, wg, wu, wd, tme, tf):
    n_tiles = tile_expert.shape[0]
    d_ff = wg.shape[2]
    nj = d_ff // tf
    assert tme % nj == 0 and tme % 8 == 0

    def jsel(i, j, te, tv):
        return jnp.where(tv[i] > 0, j, nj - 1)

    grid_spec = pltpu.PrefetchScalarGridSpec(
        num_scalar_prefetch=2,
        grid=(n_tiles, nj),
        in_specs=[
            pl.BlockSpec((1, 1, tme), lambda i, j, te, tv: (i, 0, 0), memory_space=pltpu.SMEM),
            pl.BlockSpec((1, 1, tme), lambda i, j, te, tv: (jnp.minimum(i + 1, n_tiles - 1), 0, 0),
                         memory_space=pltpu.SMEM),
            pl.BlockSpec(memory_space=pl.ANY),
            pl.BlockSpec((1, D_MODEL, tf), lambda i, j, te, tv: (te[i], 0, jsel(i, j, te, tv))),
            pl.BlockSpec((1, D_MODEL, tf), lambda i, j, te, tv: (te[i], 0, jsel(i, j, te, tv))),
            pl.BlockSpec((1, tf, D_MODEL), lambda i, j, te, tv: (te[i], jsel(i, j, te, tv), 0)),
        ],
        out_specs=pl.BlockSpec((tme, D_MODEL), lambda i, j, te, tv: (i, 0)),
        scratch_shapes=[pltpu.VMEM((2, tme, D_MODEL), F32), pltpu.VMEM((tme, D_MODEL), F32),
                        pltpu.SemaphoreType.DMA((2,))],
    )
    return pl.pallas_call(
        functools.partial(_moe_kernel, tme=tme, nj=nj),
        grid_spec=grid_spec,
        out_shape=jax.ShapeDtypeStruct((n_tiles * tme, D_MODEL), F32),
        compiler_params=pltpu.CompilerParams(dimension_semantics=("arbitrary", "arbitrary"), vmem_limit_bytes=VMEM_LIMIT),
        name="moe_experts",
    )(tile_expert, tile_valid, src_tok3, src_tok3, n_all, wg, wu, wd)


def _route_plan(route_idx, tme, n_tiles):
    n_tok = route_idx.shape[0]
    e_flat = route_idx.reshape(-1)
    onehot = (e_flat[:, None] == jnp.arange(N_EXPERTS, dtype=I32)[None, :]).astype(I32)
    csum = jnp.cumsum(onehot, axis=0)
    rank = jnp.sum(csum * onehot, axis=1) - 1
    cnt = csum[-1]
    tiles_e = (cnt + tme - 1) // tme
    tile_end = jnp.cumsum(tiles_e)
    row_off = (tile_end - tiles_e) * tme
    dst = jnp.sum(onehot * row_off[None, :], axis=1) + rank
    tile_id = jnp.arange(n_tiles, dtype=I32)
    tile_expert = jnp.minimum(jnp.sum((tile_id[:, None] >= tile_end[None, :]).astype(I32), axis=1), N_EXPERTS - 1)
    tile_valid = (tile_id < tile_end[-1]).astype(I32)
    tok_of = jnp.arange(2 * n_tok, dtype=I32) // 2
    src_tok = jnp.zeros((n_tiles * tme,), I32).at[dst].set(tok_of)
    return dst.reshape(n_tok, 2), src_tok.reshape(n_tiles, 1, tme), tile_expert.astype(I32), tile_valid


def _moe_combine_kernel(pos_ref, y_hbm, x1_ref, rw_ref, g_ref, o_ref, y_s, sem, *, tm, final_norm):
    def issue(r, c):
        p0 = pos_ref[0, 0, 2 * r]
        p1 = pos_ref[0, 0, 2 * r + 1]
        pltpu.make_async_copy(y_hbm.at[pl.ds(p0, 1), :], y_s.at[0, pl.ds(r, 1), :], sem).start()
        pltpu.make_async_copy(y_hbm.at[pl.ds(p1, 1), :], y_s.at[1, pl.ds(r, 1), :], sem).start()
        return c
    lax.fori_loop(0, tm, issue, 0, unroll=8)
    for slot in range(2):
        pltpu.make_async_copy(y_hbm.at[pl.ds(0, tm), :], y_s.at[slot], sem).wait()
    rw = rw_ref[...]
    out = x1_ref[...] + rw[:, 0:1] * y_s[0] + rw[:, 1:2] * y_s[1]
    if final_norm:
        out = _rmsnorm(out, g_ref[...])
    o_ref[...] = out


def _moe_combine_call(pos3, y_sorted, x1, rw, g_final, tm, final_norm):
    m = x1.shape[0]
    row = lambda i: (i, 0)
    return pl.pallas_call(
        functools.partial(_moe_combine_kernel, tm=tm, final_norm=final_norm),
        grid=(m // tm,),
        in_specs=[
            pl.BlockSpec((1, 1, 2 * tm), lambda i: (i, 0, 0), memory_space=pltpu.SMEM),
            pl.BlockSpec(memory_space=pl.ANY),
            pl.BlockSpec((tm, D_MODEL), row),
            pl.BlockSpec((tm, LANES), row),
            pl.BlockSpec((1, D_MODEL), lambda i: (0, 0)),
        ],
        out_specs=pl.BlockSpec((tm, D_MODEL), row),
        out_shape=jax.ShapeDtypeStruct((m, D_MODEL), F32),
        scratch_shapes=[pltpu.VMEM((2, tm, D_MODEL), F32), pltpu.SemaphoreType.DMA],
        compiler_params=pltpu.CompilerParams(dimension_semantics=("arbitrary",), vmem_limit_bytes=VMEM_LIMIT),
        name="moe_combine",
    )(pos3, y_sorted, x1, rw, g_final)


def _final_norm_kernel(x_ref, g_ref, o_ref):
    o_ref[...] = _rmsnorm(x_ref[...], g_ref[...])


def _final_norm_call(x, g, tm):
    m = x.shape[0]
    row = lambda i: (i, 0)
    return pl.pallas_call(
        _final_norm_kernel,
        grid=(m // tm,),
        in_specs=[pl.BlockSpec((tm, D_MODEL), row), pl.BlockSpec((1, D_MODEL), lambda i: (0, 0))],
        out_specs=pl.BlockSpec((tm, D_MODEL), row),
        out_shape=jax.ShapeDtypeStruct((m, D_MODEL), F32),
        compiler_params=pltpu.CompilerParams(dimension_semantics=("arbitrary",)),
        name="final_norm",
    )(x, g)


def _rope_tables(pos):
    half = HEAD_DIM // 2
    inv_freq = np.power(np.float64(ROPE_THETA), -np.arange(half, dtype=np.float64) * (2.0 / HEAD_DIM))
    ang = np.asarray(pos, np.float64)[:, None] * inv_freq[None, :]
    c, s = np.cos(ang), np.sin(ang)
    cos_t = np.concatenate([c, c, c, c], axis=1).astype(np.float32)
    sin_t = np.concatenate([-s, s, -s, s], axis=1).astype(np.float32)
    return jnp.asarray(cos_t), jnp.asarray(sin_t)


def _pack_w_in(w):
    gates0 = QKV_W + GROUP_W
    pad = jnp.zeros((w.shape[0], LANES - 2 * N_HEADS), w.dtype)
    return jnp.concatenate([w[:, :gates0], w[:, gates0 + 2 * N_HEADS:], w[:, gates0:gates0 + 2 * N_HEADS], pad],
                           axis=1)


def _lane_row(vals, offset):
    return jnp.zeros((1, LANES), F32).at[0, offset:offset + vals.shape[0]].set(vals.astype(F32))


def _pad_rows(x, rows, front=0):
    return jnp.pad(x, ((0, 0), (front, rows - front - x.shape[1]), (0, 0)))


def kernel(x_prompt, x_sample, state_conv, state_delta, cache_win_k, cache_win_v, norm_mix, w_in, conv_w, a_log,
           dt_bias, gdn_norm, w_out, norm_ffn, ffn_gate, ffn_up, ffn_down, router, moe_gate, moe_up, moe_down,
           final_norm):
    bp, seq, _ = x_prompt.shape
    bs, t_len, _ = x_sample.shape
    depth = w_in.shape[0]
    n_p, n_s = bp * seq, bs * t_len
    n_all = n_p + n_s
    tm_p, tm_s = 512, n_s
    gdn_rows = 256
    n_buf = cache_win_k.shape[2]
    n_keep = min(WINDOW_MAX, seq)

    cos_p, sin_p = _rope_tables(np.arange(seq))
    cos_s, sin_s = _rope_tables(PAST_LEN + (np.arange(n_s) % t_len))
    xp = x_prompt.reshape(n_p, D_MODEL)
    xs = x_sample.reshape(n_s, D_MODEL)
    g_final = final_norm.reshape(1, D_MODEL)

    conv_p, delta_p, wk_p, wv_p = [], [], [], []
    conv_s, delta_s = [], []
    new_k_all = new_v_all = None
    cache_kt = cache_win_k.astype(F32).transpose(0, 1, 3, 4, 2)
    cache_vt = cache_win_v.astype(F32).transpose(0, 1, 3, 4, 2)
    for l in range(depth):
        w_packed_f = _pack_w_in(w_in[l].astype(F32))
        w_packed = w_packed_f.astype(BF16)
        w_out_f = w_out[l].astype(F32)
        g_mix = norm_mix[l].reshape(1, D_MODEL)
        alog_row = _lane_row(a_log[l], N_HEADS)
        dtb_row = _lane_row(dt_bias[l], N_HEADS)
        gn_row = jnp.tile(gdn_norm[l].astype(F32), N_HEADS).reshape(1, GROUP_W)
        w_out_b = w_out[l].astype(BF16)
        g_ffn = norm_ffn[l].reshape(1, D_MODEL)
        is_moe = l % 2 == 1
        idx = l // 2

        dils = tuple(d for _, d in DILATIONS)
        qkv, z, gt, k, v, *qkv_d = _inproj_call(xp, g_mix, w_packed, cos_p, sin_p, tm_p, dils=dils)
        oa, s_bd = _gdn_prompt_call(qkv, z, gt, conv_w[l], alog_row, dtb_row, gn_row, bp, seq, gdn_rows)
        branches = [_swa_call(*qkv_d[3 * i:3 * i + 3], dil, bp, seq) for i, dil in enumerate(dils)]
        conv_p.append(qkv.reshape(bp, seq, QKV_W)[:, seq - (CONV_TAPS - 1):])
        delta_p.append(_state_from_block_diag(s_bd))
        wk_p.append(k.reshape(bp, seq, GROUP_W)[:, seq - n_keep:].reshape(bp, n_keep, N_HEADS, HEAD_DIM))
        wv_p.append(v.reshape(bp, seq, GROUP_W)[:, seq - n_keep:].reshape(bp, n_keep, N_HEADS, HEAD_DIM))

        qkv_s, z_s, gt_s, q_s, k_s, v_s = _inproj_call(xs, g_mix, w_packed_f, cos_s, sin_s, tm_s, precise=True)
        prev_pad = _pad_rows(state_conv[l].astype(F32), 8, front=8 - (CONV_TAPS - 1))
        oa_s_pad, s_bd_s = _gdn_sample_call(
            _pad_rows(qkv_s.reshape(bs, t_len, QKV_W), CHUNK), prev_pad,
            _pad_rows(z_s.reshape(bs, t_len, GROUP_W), CHUNK), _pad_rows(gt_s.reshape(bs, t_len, LANES), CHUNK),
            conv_w[l], alog_row, dtb_row, gn_row, _state_to_block_diag(state_delta[l]), t_len, 4)
        oa_s = oa_s_pad[:, :t_len].reshape(n_s, GROUP_W)
        heads = (bs, t_len, N_HEADS, HEAD_DIM)
        new_k_all, new_v_all, o_s3, l_s3 = _swa_sample_call(
            q_s.reshape(heads), k_s.reshape(heads), v_s.reshape(heads), cache_kt, cache_vt, l, new_k_all, new_v_all)
        conv_all = jnp.concatenate([state_conv[l].astype(F32), qkv_s.reshape(bs, t_len, QKV_W)], axis=1)
        conv_s.append(conv_all[:, conv_all.shape[1] - (CONV_TAPS - 1):])
        delta_s.append(_state_from_block_diag(s_bd_s))
        nd = len(DILATIONS)
        o_br_s = [o_s3[:, :, i * T_PAD:i * T_PAD + t_len].transpose(0, 2, 1, 3).reshape(n_s, GROUP_W)
                  for i in range(nd)]
        l_br_s = [jnp.pad(l_s3[:, :, i * T_PAD:i * T_PAD + t_len, 0].transpose(0, 2, 1).reshape(n_s, N_HEADS),
                          ((0, 0), (0, LANES - N_HEADS))) for i in range(nd)]

        o_br_p = [b[0] for b in branches]
        l_br_p = [b[1] for b in branches]
        if not is_moe:
            x1p, np_ = _outproj_call(oa.reshape(n_p, GROUP_W), o_br_p, l_br_p, xp, w_out_b, g_ffn, None, tm_p, BF16,
                                     dils=dils)
            x1s, ns_ = _outproj_call(oa_s, o_br_s, l_br_s, xs, w_out_f, g_ffn, None, tm_s, F32, precise=True)
            wg, wu, wd = ffn_gate[idx].astype(BF16), ffn_up[idx].astype(BF16), ffn_down[idx].astype(BF16)
            xp = _ffn_call(np_, x1p, wg, wu, wd, tm_p)
            xs = _ffn_stream_call(ns_, x1s, ffn_gate[idx].astype(F32), ffn_up[idx].astype(F32),
                                  ffn_down[idx].astype(F32), 256, precise=True)
            if l == depth - 1:
                xp = _final_norm_call(xp, g_final, tm_p)
                xs = _final_norm_call(xs, g_final, tm_s)
        else:
            router_pad = jnp.pad(router[idx].astype(F32), ((0, 0), (0, LANES - N_EXPERTS)))
            x1p, n_tok_p, ri_p, rw_p = _outproj_call(
                oa.reshape(n_p, GROUP_W), o_br_p, l_br_p, xp, w_out_b, g_ffn, router_pad, tm_p, F32, dils=dils)
            x1s, n_tok_s, ri_s, rw_s = _outproj_call(oa_s, o_br_s, l_br_s, xs, w_out_f, g_ffn, router_pad, tm_s, F32,
                                                     precise=True)
            n_tok = jnp.concatenate([n_tok_p, n_tok_s], axis=0)
            tme, tf = 1120, 512
            n_tiles = (2 * n_all + N_EXPERTS * (tme - 1)) // tme
            ridx = jnp.concatenate([ri_p[:, :2], ri_s[:, :2]], axis=0)
            dst, src_tok3, tile_expert, tile_valid = _route_plan(ridx, tme, n_tiles)
            y_sorted = _moe_call(n_tok, tile_expert, tile_valid, src_tok3, moe_gate[idx], moe_up[idx], moe_down[idx],
                                 tme, tf)
            last = l == depth - 1
            tmc_p, tmc_s = 256, n_s
            xp = _moe_combine_call(dst[:n_p].reshape(n_p // tmc_p, 1, 2 * tmc_p), y_sorted, x1p, rw_p, g_final,
                                   tmc_p, last)
            xs = _moe_combine_call(dst[n_p:].reshape(n_s // tmc_s, 1, 2 * tmc_s), y_sorted, x1s, rw_s, g_final,
                                   tmc_s, last)

    y_prompt = xp.reshape(bp, seq, D_MODEL)
    y_sample = xs.reshape(bs, t_len, D_MODEL)
    return (y_prompt, y_sample, jnp.stack(conv_p), jnp.stack(delta_p), jnp.stack(wk_p), jnp.stack(wv_p),
            jnp.stack(conv_s), jnp.stack(delta_s), new_k_all.transpose(0, 1, 4, 2, 3),
            new_v_all.transpose(0, 1, 4, 2, 3))
```

```python
import functools

import numpy as np
import jax
import jax.numpy as jnp
from jax import lax
from jax.experimental import pallas as pl
from jax.experimental.pallas import tpu as pltpu

F32 = jnp.float32
BF16 = jnp.bfloat16
I32 = jnp.int32

EPS = 1e-6
NEG_BIG = -1e30
D_MODEL = 1024
HEAD_DIM = 64
N_HEADS = 8
GROUP_W = N_HEADS * HEAD_DIM
QKV_W = 3 * GROUP_W
CONV_TAPS = 4
CHUNK = 64
HEADS_PER_PACK = 4
PACK_W = HEADS_PER_PACK * HEAD_DIM
DILATIONS = ((128, 1), (512, 4), (2048, 16))
BAND = 128
WINDOW_MAX = 2048
ROPE_THETA = 10000.0
PAST_LEN = 16384
N_EXPERTS = 8
PROJ_PACKED_W = QKV_W + 4 * GROUP_W + 128
LANES = 128
VMEM_LIMIT = 56 * 1024 * 1024


def _dot(a, b):
    return jnp.dot(a, b, preferred_element_type=F32)


def _dot_nt(a, b):
    return lax.dot_general(a, b, (((1,), (1,)), ((), ())), preferred_element_type=F32)


def _mm(a, b, precise=False):
    if precise:
        return jnp.dot(a.astype(F32), b.astype(F32), preferred_element_type=F32, precision=lax.Precision.HIGHEST)
    return jnp.dot(a.astype(BF16), b.astype(BF16), preferred_element_type=F32)


def _mm_nt(a, b, precise=False):
    if precise:
        return lax.dot_general(a.astype(F32), b.astype(F32), (((1,), (1,)), ((), ())), preferred_element_type=F32,
                               precision=lax.Precision.HIGHEST)
    return _dot_nt(a.astype(BF16), b.astype(BF16))


def _split2(x):
    hi = x.astype(BF16)
    lo = (x - hi.astype(F32)).astype(BF16)
    return hi, lo


def _split3(x):
    hi = x.astype(BF16)
    r = x - hi.astype(F32)
    mid = r.astype(BF16)
    lo = (r - mid.astype(F32)).astype(BF16)
    return hi, mid, lo


def _dot_x3(a, b):
    ah, al = _split2(a)
    bh, bl = _split2(b)
    m = a.shape[0]
    r = _dot(jnp.concatenate([ah, al], axis=0), bh)
    return r[:m] + r[m:] + _dot(ah, bl)


def _dot_exact_rhs01(x, e):
    return sum(_dot(p, e) for p in _split3(x))


def _sigmoid(x):
    return 1.0 / (1.0 + jnp.exp(-x))


def _silu(x):
    return x * _sigmoid(x)


def _softplus(x):
    return jnp.maximum(x, 0.0) + jnp.log1p(jnp.exp(-jnp.abs(x)))


def _rmsnorm(x, g):
    ms = jnp.mean(x * x, axis=-1, keepdims=True)
    return x * lax.rsqrt(ms + EPS) * g


def _head_ones(width):
    r = lax.broadcasted_iota(I32, (width, width), 0) // HEAD_DIM
    c = lax.broadcasted_iota(I32, (width, width), 1) // HEAD_DIM
    return jnp.where(r == c, 1.0, 0.0).astype(BF16)


def _head_sum(x, ones_bd, precise=True):
    if not precise:
        return _dot(x.astype(BF16), ones_bd)
    hi, lo = _split2(x)
    return _dot(hi, ones_bd) + _dot(lo, ones_bd)


def _rope(x, cos, sin):
    n = x.shape[1]
    reps = n // LANES
    c = jnp.concatenate([cos] * reps, axis=1)
    s = jnp.concatenate([sin] * reps, axis=1)
    lane = lax.broadcasted_iota(I32, x.shape, 1)
    first_half = (lane & (HEAD_DIM - 1)) < HEAD_DIM // 2
    partner = jnp.where(first_half, pltpu.roll(x, n - HEAD_DIM // 2, 1), pltpu.roll(x, HEAD_DIM // 2, 1))
    return x * c + partner * s


def _residue_perm(tm, dil):
    i = np.arange(tm)
    src = (i % (tm // dil)) * dil + i // (tm // dil)
    p = np.zeros((tm, tm), np.float32)
    p[i, src] = 1.0
    return p


def _inproj_kernel(x_ref, g_ref, w_ref, cos_ref, sin_ref, *rest, precise, dils):
    y = _rmsnorm(x_ref[...], g_ref[...])
    if not precise:
        y = y.astype(BF16)

    def mm(lo, hi):
        return _mm(y, w_ref[:, lo:hi], precise)

    cos = cos_ref[...]
    sin = sin_ref[...]
    n_perm = 0 if dils is None else sum(1 for d in dils if d > 1)
    perm_refs, outs = rest[:n_perm], rest[n_perm:]
    o = 0
    outs[0][...] = mm(o, o + QKV_W)
    o += QKV_W
    outs[1][...] = mm(o, o + GROUP_W)
    o += GROUP_W
    q = _rope(mm(o, o + GROUP_W), cos, sin)
    o += GROUP_W
    k = _rope(mm(o, o + GROUP_W), cos, sin)
    o += GROUP_W
    v = mm(o, o + GROUP_W)
    o += GROUP_W
    outs[2][...] = mm(o, o + LANES)
    if dils is None:
        outs[3][...] = q
        outs[4][...] = k
        outs[5][...] = v
        return
    outs[3][...] = k
    outs[4][...] = v
    tm = q.shape[0]
    qkv_b = [t.astype(BF16) for t in (q, k, v)]
    nxt, pi = 5, 0
    for d in dils:
        if d == 1:
            for t in qkv_b:
                outs[nxt][...] = t
                nxt += 1
            continue
        sub = tm // d
        for t in qkv_b:
            tp = _dot(perm_refs[pi][...], t).astype(BF16)
            for r in range(d):
                outs[nxt][:, r * GROUP_W:(r + 1) * GROUP_W] = tp[r * sub:(r + 1) * sub, :]
            nxt += 1
        pi += 1


def _inproj_call(x, g, w_packed, cos_tab, sin_tab, tm, precise=False, dils=None):
    m = x.shape[0]
    n_tab = cos_tab.shape[0] // tm
    row = lambda i: (i, 0)
    const = lambda i: (0, 0)
    tab = lambda i: (i % n_tab, 0)
    in_specs = [
        pl.BlockSpec((tm, D_MODEL), row),
        pl.BlockSpec((1, D_MODEL), const),
        pl.BlockSpec((D_MODEL, PROJ_PACKED_W), const),
        pl.BlockSpec((tm, LANES), tab),
        pl.BlockSpec((tm, LANES), tab),
    ]
    args = [x, g, w_packed, cos_tab, sin_tab]
    outs = [(m, QKV_W, tm, F32), (m, GROUP_W, tm, F32), (m, LANES, tm, F32)]
    if dils is None:
        outs += [(m, GROUP_W, tm, F32)] * 3
    else:
        outs += [(m, GROUP_W, tm, F32)] * 2
        for d in dils:
            outs += [(m // d, d * GROUP_W, tm // d, BF16)] * 3
            if d > 1:
                in_specs.append(pl.BlockSpec((tm, tm), const))
                args.append(jnp.asarray(_residue_perm(tm, d), BF16))
    return pl.pallas_call(
        functools.partial(_inproj_kernel, precise=precise, dils=dils),
        grid=(m // tm,),
        in_specs=in_specs,
        out_specs=[pl.BlockSpec((rows, w), row) for _, w, rows, _ in outs],
        out_shape=[jax.ShapeDtypeStruct((n, w), dt) for n, w, _, dt in outs],
        compiler_params=pltpu.CompilerParams(dimension_semantics=("arbitrary",), vmem_limit_bytes=VMEM_LIMIT),
        name="inproj",
    )(*args)


def _pack_masks():
    lane = lax.broadcasted_iota(I32, (1, PACK_W), 1)
    return [lane // HEAD_DIM == h for h in range(HEADS_PER_PACK)]


def _block_diag(x, head_masks):
    return jnp.concatenate([jnp.where(m, x, 0.0) for m in head_masks], axis=0)


def _gdn_chunk_consts():
    i = lax.broadcasted_iota(I32, (CHUNK, PACK_W), 0)
    j = lax.broadcasted_iota(I32, (CHUNK, PACK_W), 1) & (HEAD_DIM - 1)
    r = lax.broadcasted_iota(I32, (CHUNK, CHUNK), 0)
    c = lax.broadcasted_iota(I32, (CHUNK, CHUNK), 1)
    br = lax.broadcasted_iota(I32, (PACK_W, PACK_W), 0) // HEAD_DIM
    bc = lax.broadcasted_iota(I32, (PACK_W, PACK_W), 1) // HEAD_DIM
    return dict(
        head_masks=_pack_masks(),
        lower=i >= j,
        strict=i > j,
        upper_f=jnp.where(i <= j, 1.0, 0.0),
        eye=jnp.where(i == j, 1.0, 0.0),
        tri=jnp.where(r >= c, 1.0, 0.0).astype(BF16),
        block=br == bc,
    )


def _gdn_chunk(qn, kn, v, bw, gw, state, cst, precise):
    hm = cst["head_masks"]
    gcum = sum(_dot(cst["tri"], p) for p in _split3(gw))
    grow = jnp.sum(gw * cst["upper_f"], axis=0, keepdims=True)
    diff = gcum - grow
    decay = jnp.where(cst["lower"], jnp.exp(jnp.where(cst["lower"], diff, 0.0)), 0.0)
    kq = _mm_nt(jnp.concatenate([kn, qn], axis=0), _block_diag(kn, hm), precise)
    kk, qk = kq[:CHUNK], kq[CHUNK:]
    nmat = bw * kk * jnp.where(cst["strict"], decay, 0.0)
    p = -nmat
    y = cst["eye"] + p
    q = _dot_x3(p, _block_diag(p, hm))
    n_sq = CHUNK.bit_length() - 1
    for s in range(1, n_sq):
        q_bd = _block_diag(q, hm)
        if s < n_sq - 1:
            both = _dot_x3(jnp.concatenate([y, q], axis=0), q_bd)
            y = y + both[:CHUNK]
            q = both[CHUNK:]
        else:
            y = y + _dot_x3(y, q_bd)
    egc = jnp.exp(gcum)
    rhs = jnp.concatenate([_block_diag(bw * v, hm), _block_diag(bw * egc * kn, hm)], axis=1)
    sol = _mm(y, rhs, precise)
    u, wk = sol[:, :PACK_W], sol[:, PACK_W:]
    aqk = qk * decay
    qd = qn * egc
    glast = gcum[CHUNK - 1:CHUNK, :]
    kd = kn * jnp.exp(glast - gcum)
    m1 = _mm(jnp.concatenate([wk, qd], axis=0), state, precise)
    w = u - m1[:CHUNK]
    o = m1[CHUNK:] + _mm(aqk, _block_diag(w, hm), precise)
    inc = _mm(jnp.transpose(kd), w, precise)
    new_state = state * jnp.exp(glast) + jnp.where(cst["block"], inc, 0.0)
    return o, new_state


def _bmm_dn(a, b, dn, precise):
    if not precise:
        return lax.dot_general(a.astype(BF16), b.astype(BF16), dn, preferred_element_type=F32)
    ah, al = _split2(a.astype(F32))
    bh, bl = _split2(b.astype(F32))
    m = a.shape[1]
    r = lax.dot_general(jnp.concatenate([ah, al], axis=1), bh, dn, preferred_element_type=F32)
    return r[:, :m] + r[:, m:] + lax.dot_general(ah, bl, dn, preferred_element_type=F32)


def _bmm(a, b, precise=False):
    return _bmm_dn(a, b, (((2,), (1,)), ((0,), (0,))), precise)


def _bmm_nt(a, b, precise=False):
    return _bmm_dn(a, b, (((2,), (2,)), ((0,), (0,))), precise)


def _block_diag_b(x, head_masks):
    return jnp.concatenate([jnp.where(m, x, 0.0) for m in head_masks], axis=1)


def _gdn_local(qn, kn, v, bw, gw, cst, precise):
    hm = cst["head_masks"]
    nb = qn.shape[0]
    tri = jnp.broadcast_to(cst["tri"], (nb, CHUNK, CHUNK))
    gcum = sum(_bmm(tri, p) for p in _split3(gw))
    grow = jnp.sum(gw * cst["upper_f"], axis=1, keepdims=True)
    diff = gcum - grow
    decay = jnp.where(cst["lower"], jnp.exp(jnp.where(cst["lower"], diff, 0.0)), 0.0)
    kq = _bmm_nt(jnp.concatenate([kn, qn], axis=1), _block_diag_b(kn, hm), precise)
    kk, qk = kq[:, :CHUNK], kq[:, CHUNK:]
    nmat = bw * kk * jnp.where(cst["strict"], decay, 0.0)
    p = -nmat
    y = cst["eye"] + p
    q = _bmm(p, _block_diag_b(p, hm), precise)
    n_sq = CHUNK.bit_length() - 1
    for s in range(1, n_sq):
        q_bd = _block_diag_b(q, hm)
        if s < n_sq - 1:
            both = _bmm(jnp.concatenate([y, q], axis=1), q_bd, precise)
            y = y + both[:, :CHUNK]
            q = both[:, CHUNK:]
        else:
            y = y + _bmm(y, q_bd, precise)
    egc = jnp.exp(gcum)
    rhs = jnp.concatenate([_block_diag_b(bw * v, hm), _block_diag_b(bw * egc * kn, hm)], axis=2)
    sol = _bmm(y, rhs, precise)
    glast = gcum[:, CHUNK - 1:CHUNK, :]
    return (sol[:, :, :PACK_W], sol[:, :, PACK_W:], qk * decay, qn * egc, kn * jnp.exp(glast - gcum),
            jnp.exp(glast))


def _gdn_scan_step(u, wk, aqk, qd, kd, gl, state, cst, precise):
    hm = cst["head_masks"]
    m1 = _bmm(jnp.concatenate([wk, qd], axis=1), state, precise)
    w = u - m1[:, :CHUNK]
    o = m1[:, CHUNK:] + _bmm(aqk, _block_diag_b(w, hm), precise)
    inc = _bmm(jnp.swapaxes(kd, 1, 2), w, precise)
    return o, state * gl + jnp.where(cst["block"], inc, 0.0)


def _gdn_kernel(*refs, nb, rows, carry, n_valid, precise):
    if carry:
        (qkv_ref, prev_ref, z_ref, gt_ref, cw_ref, al_ref, db_ref, gn_ref, o_ref, sout_ref, xp_s, st_s) = refs
        s0_ref = None
    else:
        (qkv_ref, prev_ref, z_ref, gt_ref, cw_ref, al_ref, db_ref, gn_ref, s0_ref, o_ref, sout_ref, xp_s, st_s) = refs
    step = pl.program_id(0)
    ones_bd = _head_ones(GROUP_W)
    lane = lax.broadcasted_iota(I32, (1, LANES), 1)
    er = lax.broadcasted_iota(I32, (LANES, 2 * GROUP_W), 0)
    ec = lax.broadcasted_iota(I32, (LANES, 2 * GROUP_W), 1)
    expand = jnp.where(((ec < GROUP_W) & (er == ec // HEAD_DIM))
                       | ((ec >= GROUP_W) & (er == N_HEADS + (ec - GROUP_W) // HEAD_DIM)), 1.0, 0.0).astype(BF16)
    cw = cw_ref[...]

    if carry:
        @pl.when(step == 0)
        def _():
            st_s[...] = jnp.zeros_like(st_s)
    else:
        st_s[...] = s0_ref[...].reshape(st_s.shape)

    per_seq = []
    for b in range(nb):
        prev = prev_ref[b]
        if carry:
            prev = jnp.where(step == 0, 0.0, prev)
        xp_s[b, 0:8, :] = prev
        xp_s[b, 8:8 + rows, :] = qkv_ref[b]
        acc = cw[CONV_TAPS - 1:CONV_TAPS, :] * xp_s[b, 8:8 + rows, :]
        for t in range(1, CONV_TAPS):
            acc = acc + cw[CONV_TAPS - 1 - t:CONV_TAPS - t, :] * xp_s[b, 8 - t:8 - t + rows, :]
        cq = _silu(acc)
        q = cq[:, :GROUP_W]
        k = cq[:, GROUP_W:2 * GROUP_W]
        v = cq[:, 2 * GROUP_W:]
        qn = q * lax.rsqrt(_head_sum(q * q, ones_bd, precise) + EPS) * (HEAD_DIM ** -0.5)
        kn = k * lax.rsqrt(_head_sum(k * k, ones_bd, precise) + EPS)
        t_raw = gt_ref[b]
        beta = _sigmoid(t_raw)
        gdec = -jnp.exp(al_ref[...]) * _softplus(t_raw + db_ref[...])
        wide = _dot_exact_rhs01(jnp.where(lane < N_HEADS, beta, gdec), expand)
        bw = wide[:, :GROUP_W]
        gw = wide[:, GROUP_W:]
        if n_valid < rows:
            live = lax.broadcasted_iota(I32, (rows, 1), 0) < n_valid
            bw = jnp.where(live, bw, 0.0)
            gw = jnp.where(live, gw, 0.0)
            qn = jnp.where(live, qn, 0.0)
            kn = jnp.where(live, kn, 0.0)
            v = jnp.where(live, v, 0.0)
        per_seq.append((qn, kn, v, bw, gw))

    cst = _gdn_chunk_consts()
    n_chunks = rows // CHUNK
    n_pack = N_HEADS // HEADS_PER_PACK

    def instances(idx):
        return jnp.stack([per_seq[b][idx][c * CHUNK:(c + 1) * CHUNK, g * PACK_W:(g + 1) * PACK_W]
                          for c in range(n_chunks) for b in range(nb) for g in range(n_pack)], axis=0)

    u, wk, aqk, qd, kd, gl = _gdn_local(*(instances(i) for i in range(5)), cst, precise)
    n_chain = nb * n_pack
    state = st_s[...]
    o_chunks = []
    for c in range(n_chunks):
        sl = slice(c * n_chain, (c + 1) * n_chain)
        o_c, state = _gdn_scan_step(u[sl], wk[sl], aqk[sl], qd[sl], kd[sl], gl[sl], state, cst, precise)
        o_chunks.append(o_c)
    st_s[...] = state

    for b in range(nb):
        o = jnp.concatenate(
            [jnp.concatenate([o_chunks[c][b * n_pack + g] for g in range(n_pack)], axis=1) for c in range(n_chunks)],
            axis=0)
        ms = _head_sum(o * o, ones_bd, precise) * (1.0 / HEAD_DIM)
        o_ref[b] = (o * lax.rsqrt(ms + EPS) * gn_ref[...] * _silu(z_ref[b])).astype(o_ref.dtype)

    if carry:
        @pl.when(step == pl.num_programs(0) - 1)
        def _():
            sout_ref[...] = state.reshape(sout_ref.shape)
    else:
        sout_ref[...] = state.reshape(sout_ref.shape)


def _gdn_scratch(nb, rows):
    return [pltpu.VMEM((nb, rows + 8, QKV_W), F32),
            pltpu.VMEM((nb * (N_HEADS // HEADS_PER_PACK), PACK_W, PACK_W), F32)]


def _gdn_prompt_call(qkv, z, gt, conv_w, alog_row, dtb_row, gn_row, batch, seq, rows):
    qkv3 = qkv.reshape(batch, seq, QKV_W)
    z3 = z.reshape(batch, seq, GROUP_W)
    gt3 = gt.reshape(batch, seq, LANES)
    blk = lambda i: (0, i, 0)
    const2 = lambda i: (0, 0)
    n_pack = N_HEADS // HEADS_PER_PACK
    return pl.pallas_call(
        functools.partial(_gdn_kernel, nb=batch, rows=rows, carry=True, n_valid=rows, precise=False),
        grid=(seq // rows,),
        in_specs=[
            pl.BlockSpec((batch, rows, QKV_W), blk),
            pl.BlockSpec((batch, 8, QKV_W), lambda i: (0, jnp.maximum(i * (rows // 8) - 1, 0), 0)),
            pl.BlockSpec((batch, rows, GROUP_W), blk),
            pl.BlockSpec((batch, rows, LANES), blk),
            pl.BlockSpec((CONV_TAPS, QKV_W), const2),
            pl.BlockSpec((1, LANES), const2),
            pl.BlockSpec((1, LANES), const2),
            pl.BlockSpec((1, GROUP_W), const2),
        ],
        out_specs=[
            pl.BlockSpec((batch, rows, GROUP_W), blk),
            pl.BlockSpec((batch, n_pack, PACK_W, PACK_W), lambda i: (0, 0, 0, 0)),
        ],
        out_shape=[
            jax.ShapeDtypeStruct((batch, seq, GROUP_W), BF16),
            jax.ShapeDtypeStruct((batch, n_pack, PACK_W, PACK_W), F32),
        ],
        scratch_shapes=_gdn_scratch(batch, rows),
        compiler_params=pltpu.CompilerParams(dimension_semantics=("arbitrary",), vmem_limit_bytes=VMEM_LIMIT),
        name="gdn_prompt",
    )(qkv3, qkv3, z3, gt3, conv_w, alog_row, dtb_row, gn_row)


def _gdn_sample_call(qkv_pad, prev_pad, z_pad, gt_pad, conv_w, alog_row, dtb_row, gn_row, s0_bd, n_valid, nb):
    n_seq = qkv_pad.shape[0]
    blk = lambda i: (i, 0, 0)
    blk4 = lambda i: (i, 0, 0, 0)
    const2 = lambda i: (0, 0)
    n_pack = N_HEADS // HEADS_PER_PACK
    return pl.pallas_call(
        functools.partial(_gdn_kernel, nb=nb, rows=CHUNK, carry=False, n_valid=n_valid, precise=True),
        grid=(n_seq // nb,),
        in_specs=[
            pl.BlockSpec((nb, CHUNK, QKV_W), blk),
            pl.BlockSpec((nb, 8, QKV_W), blk),
            pl.BlockSpec((nb, CHUNK, GROUP_W), blk),
            pl.BlockSpec((nb, CHUNK, LANES), blk),
            pl.BlockSpec((CONV_TAPS, QKV_W), const2),
            pl.BlockSpec((1, LANES), const2),
            pl.BlockSpec((1, LANES), const2),
            pl.BlockSpec((1, GROUP_W), const2),
            pl.BlockSpec((nb, n_pack, PACK_W, PACK_W), blk4),
        ],
        out_specs=[
            pl.BlockSpec((nb, CHUNK, GROUP_W), blk),
            pl.BlockSpec((nb, n_pack, PACK_W, PACK_W), blk4),
        ],
        out_shape=[
            jax.ShapeDtypeStruct((n_seq, CHUNK, GROUP_W), F32),
            jax.ShapeDtypeStruct((n_seq, n_pack, PACK_W, PACK_W), F32),
        ],
        scratch_shapes=_gdn_scratch(nb, CHUNK),
        compiler_params=pltpu.CompilerParams(dimension_semantics=("arbitrary",), vmem_limit_bytes=VMEM_LIMIT),
        name="gdn_sample",
    )(qkv_pad, prev_pad, z_pad, gt_pad, conv_w, alog_row, dtb_row, gn_row, s0_bd)


def _state_to_block_diag(s):
    n = s.shape[0]
    n_pack = N_HEADS // HEADS_PER_PACK
    s5 = s.reshape(n, n_pack, HEADS_PER_PACK, HEAD_DIM, HEAD_DIM).astype(F32)
    eye = jnp.eye(HEADS_PER_PACK, dtype=F32)
    return jnp.einsum("bghde,hk->bghdke", s5, eye).reshape(n, n_pack, PACK_W, PACK_W)


def _state_from_block_diag(s_bd):
    n = s_bd.shape[0]
    n_pack = N_HEADS // HEADS_PER_PACK
    s6 = s_bd.reshape(n, n_pack, HEADS_PER_PACK, HEAD_DIM, HEADS_PER_PACK, HEAD_DIM)
    diag = jnp.stack([s6[:, :, h, :, h, :] for h in range(HEADS_PER_PACK)], axis=2)
    return diag.reshape(n, N_HEADS, HEAD_DIM, HEAD_DIM)


def _swa_kernel(q_ref, kc_ref, kp_ref, vc_ref, vp_ref, o_ref, l_ref, *, tq):
    first_block = pl.program_id(2) == 0
    n_sub = tq // BAND
    lane = lax.broadcasted_iota(I32, (1, LANES), 1)
    low = lane < HEAD_DIM
    qi = lax.broadcasted_iota(I32, (BAND, 2 * BAND), 0)
    kj = lax.broadcasted_iota(I32, (BAND, 2 * BAND), 1)
    dist = qi + BAND - kj
    band = (dist >= 0) & (dist <= BAND)
    band_first = band & ((kj >= BAND) | jnp.logical_not(first_block))
    lane_l = lax.broadcasted_iota(I32, (BAND, LANES), 1)
    lse_acc = [jnp.zeros((BAND, LANES), F32) for _ in range(n_sub)]
    for p in range(GROUP_W // LANES):
        ls = slice(p * LANES, (p + 1) * LANES)
        q_pair = (q_ref[:, ls].astype(F32) * (HEAD_DIM ** -0.5)).astype(BF16)
        k_all = jnp.concatenate([kp_ref[:, ls], kc_ref[:, ls]], axis=0)
        v_all = jnp.concatenate([vp_ref[:, ls], vc_ref[:, ls]], axis=0)
        o_sub = [jnp.zeros((BAND, LANES), F32) for _ in range(n_sub)]
        for hh in range(2):
            hmask = low if hh == 0 else jnp.logical_not(low)
            k_h = jnp.where(hmask, k_all, jnp.zeros_like(k_all))
            v_h = jnp.where(hmask, v_all, jnp.zeros_like(v_all))
            head = 2 * p + hh
            for s in range(n_sub):
                sc = _dot_nt(q_pair[s * BAND:(s + 1) * BAND], k_h[s * BAND:s * BAND + 2 * BAND])
                valid = band_first if s == 0 else band
                sc = jnp.where(valid, sc, NEG_BIG)
                m = jnp.max(sc, axis=1, keepdims=True)
                pm = jnp.exp(sc - m)
                den = jnp.sum(pm, axis=1, keepdims=True)
                num = _dot(pm.astype(BF16), v_h[s * BAND:s * BAND + 2 * BAND])
                o_sub[s] = o_sub[s] + num / den
                lse_acc[s] = lse_acc[s] + jnp.where(lane_l == head, m + jnp.log(den), 0.0)
        for s in range(n_sub):
            o_ref[s * BAND:(s + 1) * BAND, ls] = o_sub[s].astype(o_ref.dtype)
    for s in range(n_sub):
        l_ref[s * BAND:(s + 1) * BAND, :] = lse_acc[s]


def _swa_call(qv, kv, vv, dil, batch, seq):
    sub_len = seq // dil
    tq = min(512, sub_len)
    nq = sub_len // tq
    rows = batch * seq // dil
    cur = lambda b, r, i: (b * nq + i, r)
    prv = lambda b, r, i: (jnp.maximum((b * nq + i) * (tq // BAND) - 1, 0), r)
    return pl.pallas_call(
        functools.partial(_swa_kernel, tq=tq),
        grid=(batch, dil, nq),
        in_specs=[
            pl.BlockSpec((tq, GROUP_W), cur),
            pl.BlockSpec((tq, GROUP_W), cur),
            pl.BlockSpec((BAND, GROUP_W), prv),
            pl.BlockSpec((tq, GROUP_W), cur),
            pl.BlockSpec((BAND, GROUP_W), prv),
        ],
        out_specs=[pl.BlockSpec((tq, GROUP_W), cur), pl.BlockSpec((tq, LANES), cur)],
        out_shape=[jax.ShapeDtypeStruct((rows, dil * GROUP_W), BF16),
                   jax.ShapeDtypeStruct((rows, dil * LANES), F32)],
        compiler_params=pltpu.CompilerParams(dimension_semantics=("arbitrary",) * 3, vmem_limit_bytes=VMEM_LIMIT),
        name=f"swa_d{dil}",
    )(qv, kv, kv, vv, vv)


T_PAD = 8


def _swa_sample_kernel(*refs, t_len, n_buf, n_alias):
    q_ref, kn_ref, vn_ref, ktail_ref, vtail_ref, kc_ref, vc_ref = refs[:7]
    nk_ref, nv_ref, o_ref, l_ref = refs[7 + n_alias:]
    t_row = lax.broadcasted_iota(I32, (T_PAD, n_buf), 0)
    delta_c = n_buf + t_row - lax.broadcasted_iota(I32, (T_PAD, n_buf), 1)
    t_row_n = lax.broadcasted_iota(I32, (T_PAD, LANES), 0)
    col_n = lax.broadcasted_iota(I32, (T_PAD, LANES), 1)
    delta_n = t_row_n - col_n
    lane = lax.broadcasted_iota(I32, (HEAD_DIM, LANES), 1)
    masks = []
    for window, dil in DILATIONS:
        masks.append(((delta_c >= 0) & ((delta_c & (dil - 1)) == 0) & (delta_c <= window),
                      (col_n < t_len) & (delta_n >= 0) & ((delta_n & (dil - 1)) == 0) & (delta_n <= window)))

    def x3(a_parts, b_parts, nt):
        (ah, al), (bh, bl) = a_parts, b_parts
        f = _dot_nt if nt else _dot
        r = f(jnp.concatenate([ah, al], axis=0), bh)
        return r[:T_PAD] + r[T_PAD:] + f(ah, bl)

    for h in range(N_HEADS):
        kt = kc_ref[h]
        vt = vc_ref[h]
        for src, tail_ref, dst in ((kt, ktail_ref, nk_ref), (vt, vtail_ref, nv_ref)):
            rolled = pltpu.roll(src, n_buf - t_len, 1)
            dst[h, :, 0:n_buf - LANES] = rolled[:, 0:n_buf - LANES]
            dst[h, :, n_buf - LANES:n_buf] = jnp.where(lane >= LANES - t_len, tail_ref[h], rolled[:, n_buf - LANES:])
        q_p = _split2(q_ref[h] * (HEAD_DIM ** -0.5))
        s_c = x3(q_p, _split2(kt), False)
        s_n = x3(q_p, _split2(kn_ref[h]), True)
        vt_p = _split2(vt)
        vn_p = _split2(vn_ref[h])
        for bi, (valid_c, valid_n) in enumerate(masks):
            sc = jnp.where(valid_c, s_c, NEG_BIG)
            sn = jnp.where(valid_n, s_n, NEG_BIG)
            m = jnp.maximum(jnp.max(sc, axis=1, keepdims=True), jnp.max(sn, axis=1, keepdims=True))
            pc = jnp.exp(sc - m)
            pn = jnp.exp(sn - m)
            den = jnp.sum(pc, axis=1, keepdims=True) + jnp.sum(pn, axis=1, keepdims=True)
            num = x3(_split2(pc), vt_p, True) + x3(_split2(pn), vn_p, False)
            o_ref[h, bi * T_PAD:(bi + 1) * T_PAD, :] = num / den
            l_ref[h, bi * T_PAD:(bi + 1) * T_PAD, :] = jnp.broadcast_to(m + jnp.log(den), (T_PAD, LANES))


def _swa_sample_call(q, k_new, v_new, cache_kt, cache_vt, layer, new_kt_all, new_vt_all):
    nbatch, t_len = q.shape[:2]
    n_buf = cache_kt.shape[-1]
    nd = len(DILATIONS)
    assert t_len <= T_PAD and n_buf % LANES == 0 and n_buf >= max(w for w, _ in DILATIONS)

    def head_major(x, rows):
        return jnp.pad(x.transpose(0, 2, 1, 3), ((0, 0), (0, 0), (0, rows - t_len), (0, 0)))

    def tail(x):
        return jnp.pad(x.transpose(0, 2, 3, 1), ((0, 0), (0, 0), (0, 0), (LANES - t_len, 0)))

    def spec(rows, width):
        return pl.BlockSpec((None, N_HEADS, rows, width), lambda b: (b, 0, 0, 0))

    cache = pl.BlockSpec((None, None, N_HEADS, HEAD_DIM, n_buf), lambda b: (layer, b, 0, 0, 0))
    in_specs = [spec(T_PAD, HEAD_DIM), spec(LANES, HEAD_DIM), spec(LANES, HEAD_DIM), spec(HEAD_DIM, LANES),
                spec(HEAD_DIM, LANES), cache, cache]
    args = [head_major(q, T_PAD), head_major(k_new, LANES), head_major(v_new, LANES), tail(k_new), tail(v_new),
            cache_kt, cache_vt]
    aliases = {}
    if new_kt_all is not None:
        in_specs += [pl.BlockSpec(memory_space=pl.ANY)] * 2
        args += [new_kt_all, new_vt_all]
        aliases = {7: 0, 8: 1}
    return pl.pallas_call(
        functools.partial(_swa_sample_kernel, t_len=t_len, n_buf=n_buf, n_alias=len(aliases)),
        grid=(nbatch,),
        in_specs=in_specs,
        out_specs=[cache, cache, spec(nd * T_PAD, HEAD_DIM), spec(nd * T_PAD, LANES)],
        out_shape=[jax.ShapeDtypeStruct(cache_kt.shape, F32), jax.ShapeDtypeStruct(cache_vt.shape, F32),
                   jax.ShapeDtypeStruct((nbatch, N_HEADS, nd * T_PAD, HEAD_DIM), F32),
                   jax.ShapeDtypeStruct((nbatch, N_HEADS, nd * T_PAD, LANES), F32)],
        input_output_aliases=aliases,
        compiler_params=pltpu.CompilerParams(dimension_semantics=("arbitrary",), vmem_limit_bytes=VMEM_LIMIT),
        name="swa_sample",
    )(*args)


def _outproj_kernel(*refs, route, n_dtype, precise, dils):
    oa_ref = refs[0]
    o_refs = refs[1:4]
    l_refs = refs[4:7]
    x_ref, w_ref, g_ref = refs[7:10]
    nxt = 10
    if route:
        r_ref = refs[nxt]
        nxt += 1
    n_perm = sum(1 for d in dils if d > 1)
    perm_refs = refs[nxt:nxt + n_perm]
    nxt += n_perm
    if route:
        x1_ref, n_ref, ri_ref, rw_ref = refs[nxt:]
    else:
        x1_ref, n_ref = refs[nxt:]

    def token_order(ref, d, pi, width, exact_f32):
        if d == 1:
            return ref[...]
        stacked = jnp.concatenate([ref[:, r * width:(r + 1) * width] for r in range(d)], axis=0)
        pt = perm_refs[pi][...]
        if exact_f32:
            return sum(_dot(pt, part) for part in _split3(stacked))
        return _dot(pt, stacked)

    o_vals, lses, pi = [], [], 0
    for d, o_ref, l_ref in zip(dils, o_refs, l_refs):
        o_vals.append(token_order(o_ref, d, pi, GROUP_W, False))
        lses.append(token_order(l_ref, d, pi, LANES, True))
        pi += 1 if d > 1 else 0
    mx = jnp.maximum(jnp.maximum(lses[0], lses[1]), lses[2])
    es = [jnp.exp(l - mx) for l in lses]
    inv = 1.0 / (es[0] + es[1] + es[2])
    er = lax.broadcasted_iota(I32, (LANES, GROUP_W), 0)
    ec = lax.broadcasted_iota(I32, (LANES, GROUP_W), 1)
    expand = jnp.where(er == ec // HEAD_DIM, 1.0, 0.0).astype(BF16)
    ob = None
    for e, o_val in zip(es, o_vals):
        hi, lo = _split2(e * inv)
        term = (_dot(hi, expand) + _dot(lo, expand)) * o_val
        ob = term if ob is None else ob + term
    mixed = _mm(oa_ref[...], w_ref[0:GROUP_W, :], precise) + _mm(ob, w_ref[GROUP_W:, :], precise)
    x1 = x_ref[...] + mixed
    x1_ref[...] = x1
    n = _rmsnorm(x1, g_ref[...])
    n_ref[...] = n.astype(n_dtype)
    if route:
        logits = _mm(n, r_ref[...], True) if precise else _dot_x3(n, r_ref[...])
        lane = lax.broadcasted_iota(I32, logits.shape, 1)
        logits = jnp.where(lane < N_EXPERTS, logits, NEG_BIG)
        m1 = jnp.max(logits, axis=1, keepdims=True)
        i1 = jnp.min(jnp.where(logits == m1, lane, LANES), axis=1, keepdims=True)
        rest = jnp.where(lane == i1, NEG_BIG, logits)
        m2 = jnp.max(rest, axis=1, keepdims=True)
        i2 = jnp.min(jnp.where(rest == m2, lane, LANES), axis=1, keepdims=True)
        e2 = jnp.exp(m2 - m1)
        w1 = 1.0 / (1.0 + e2)
        w2 = e2 * w1
        ri_ref[...] = jnp.where(lane == 0, i1, jnp.where(lane == 1, i2, 0))
        rw_ref[...] = jnp.where(lane == 0, w1, jnp.where(lane == 1, w2, 0.0))


def _outproj_call(oa, o_br, l_br, x, w_out, g, router_pad, tm, n_dtype, precise=False, dils=(1, 1, 1)):
    m = x.shape[0]
    route = router_pad is not None
    row = lambda i: (i, 0)
    const = lambda i: (0, 0)
    in_specs = ([pl.BlockSpec((tm, GROUP_W), row)]
                + [pl.BlockSpec((tm // d, d * GROUP_W), row) for d in dils]
                + [pl.BlockSpec((tm // d, d * LANES), row) for d in dils]
                + [pl.BlockSpec((tm, D_MODEL), row), pl.BlockSpec((D_MODEL, D_MODEL), const),
                   pl.BlockSpec((1, D_MODEL), const)])
    args = [oa, *o_br, *l_br, x, w_out, g]
    if route:
        in_specs.append(pl.BlockSpec((D_MODEL, LANES), const))
        args.append(router_pad)
    for d in dils:
        if d > 1:
            in_specs.append(pl.BlockSpec((tm, tm), const))
            args.append(jnp.asarray(_residue_perm(tm, d).T, BF16))
    out_specs = [pl.BlockSpec((tm, D_MODEL), row)] * 2
    out_shape = [jax.ShapeDtypeStruct((m, D_MODEL), F32), jax.ShapeDtypeStruct((m, D_MODEL), n_dtype)]
    if route:
        out_specs += [pl.BlockSpec((tm, LANES), row)] * 2
        out_shape += [jax.ShapeDtypeStruct((m, LANES), I32), jax.ShapeDtypeStruct((m, LANES), F32)]
    return pl.pallas_call(
        functools.partial(_outproj_kernel, route=route, n_dtype=n_dtype, precise=precise, dils=tuple(dils)),
        grid=(m // tm,),
        in_specs=in_specs,
        out_specs=out_specs,
        out_shape=out_shape,
        compiler_params=pltpu.CompilerParams(dimension_semantics=("arbitrary",), vmem_limit_bytes=VMEM_LIMIT),
        name="outproj",
    )(*args)


def _ffn_kernel(n_ref, x1_ref, wg_ref, wu_ref, wd_ref, o_ref, *, f_chunk, precise):
    n = n_ref[...]
    d_ff = wg_ref.shape[1]
    acc = x1_ref[...]
    for f0 in range(0, d_ff, f_chunk):
        h = _silu(_mm(n, wg_ref[:, f0:f0 + f_chunk], precise)) * _mm(n, wu_ref[:, f0:f0 + f_chunk], precise)
        acc = acc + _mm(h, wd_ref[f0:f0 + f_chunk, :], precise)
    o_ref[...] = acc


def _ffn_call(n, x1, wg, wu, wd, tm, precise=False):
    m = n.shape[0]
    d_ff = wg.shape[1]
    f_chunk = d_ff // 2 if (d_ff // 2) % LANES == 0 else d_ff
    row = lambda i: (i, 0)
    const = lambda i: (0, 0)
    return pl.pallas_call(
        functools.partial(_ffn_kernel, f_chunk=f_chunk, precise=precise),
        grid=(m // tm,),
        in_specs=[pl.BlockSpec((tm, D_MODEL), row), pl.BlockSpec((tm, D_MODEL), row),
                  pl.BlockSpec((D_MODEL, d_ff), const), pl.BlockSpec((D_MODEL, d_ff), const),
                  pl.BlockSpec((d_ff, D_MODEL), const)],
        out_specs=pl.BlockSpec((tm, D_MODEL), row),
        out_shape=jax.ShapeDtypeStruct((m, D_MODEL), F32),
        compiler_params=pltpu.CompilerParams(dimension_semantics=("arbitrary",), vmem_limit_bytes=VMEM_LIMIT),
        name="ffn_dense",
    )(n, x1, wg, wu, wd)


def _ffn_stream_kernel(n_ref, x1_ref, wg_ref, wu_ref, wd_ref, o_ref, *, precise):
    @pl.when(pl.program_id(0) == 0)
    def _():
        o_ref[...] = x1_ref[...]

    n = n_ref[...]
    h = _silu(_mm(n, wg_ref[...], precise)) * _mm(n, wu_ref[...], precise)
    o_ref[...] += _mm(h, wd_ref[...], precise)


def _ffn_stream_call(n, x1, wg, wu, wd, tf, precise):
    m = n.shape[0]
    d_ff = wg.shape[1]
    const = lambda j: (0, 0)
    return pl.pallas_call(
        functools.partial(_ffn_stream_kernel, precise=precise),
        grid=(d_ff // tf,),
        in_specs=[pl.BlockSpec((m, D_MODEL), const), pl.BlockSpec((m, D_MODEL), const),
                  pl.BlockSpec((D_MODEL, tf), lambda j: (0, j)), pl.BlockSpec((D_MODEL, tf), lambda j: (0, j)),
                  pl.BlockSpec((tf, D_MODEL), lambda j: (j, 0))],
        out_specs=pl.BlockSpec((m, D_MODEL), const),
        out_shape=jax.ShapeDtypeStruct((m, D_MODEL), F32),
        compiler_params=pltpu.CompilerParams(dimension_semantics=("arbitrary",), vmem_limit_bytes=VMEM_LIMIT),
        name="ffn_stream",
    )(n, x1, wg, wu, wd)


def _moe_kernel(te_ref, tv_ref, src_ref, nxt_ref, n_hbm, wg_ref, wu_ref, wd_ref, y_ref, x_s, xb_s, acc_s, sems, *,
                tme, nj):
    i = pl.program_id(0)
    j = pl.program_id(1)
    n_tiles = pl.num_programs(0)
    slot = i % 2
    share = tme // nj

    def tile_wait(s):
        pltpu.make_async_copy(n_hbm.at[pl.ds(0, tme), :], x_s.at[s], sems.at[s]).wait()

    @pl.when(tv_ref[i] > 0)
    def _():
        @pl.when((i == 0) & (j == 0))
        def _():
            def issue(r, c):
                pltpu.make_async_copy(n_hbm.at[pl.ds(src_ref[0, 0, r], 1), :], x_s.at[0, pl.ds(r, 1), :],
                                      sems.at[0]).start()
                return c
            lax.fori_loop(0, tme, issue, 0, unroll=8)

        @pl.when(j == 0)
        def _():
            tile_wait(slot)
            xb_s[...] = x_s[slot].astype(BF16)
            acc_s[...] = jnp.zeros_like(acc_s)

        x = xb_s[...]
        h = _silu(_mm(x, wg_ref[0])) * _mm(x, wu_ref[0])
        acc_s[...] += _mm(h, wd_ref[0])
        for k in range(share):
            r = j * share + k
            pltpu.make_async_copy(n_hbm.at[pl.ds(nxt_ref[0, 0, r], 1), :], x_s.at[1 - slot, pl.ds(r, 1), :],
                                  sems.at[1 - slot]).start()

        @pl.when(j == nj - 1)
        def _():
            y_ref[...] = acc_s[...]

            @pl.when(tv_ref[jnp.minimum(i + 1, n_tiles - 1)] * (i + 1 < n_tiles).astype(I32) == 0)
            def _():
                tile_wait(1 - slot)

    @pl.when((tv_ref[i] == 0) & (j == nj - 1))
    def _():
        y_ref[...] = jnp.zeros_like(y_ref)


def _moe_call(n_all, tile_expert, tile_valid, src_tok3, wg, wu, wd, tme, tf):
    n_tiles = tile_expert.shape[0]
    d_ff = wg.shape[2]
    nj = d_ff // tf
    assert tme % nj == 0 and tme % 8 == 0

    def jsel(i, j, te, tv):
        return jnp.where(tv[i] > 0, j, nj - 1)

    grid_spec = pltpu.PrefetchScalarGridSpec(
        num_scalar_prefetch=2,
        grid=(n_tiles, nj),
        in_specs=[
            pl.BlockSpec((1, 1, tme), lambda i, j, te, tv: (i, 0, 0), memory_space=pltpu.SMEM),
            pl.BlockSpec((1, 1, tme), lambda i, j, te, tv: (jnp.minimum(i + 1, n_tiles - 1), 0, 0),
                         memory_space=pltpu.SMEM),
            pl.BlockSpec(memory_space=pl.ANY),
            pl.BlockSpec((1, D_MODEL, tf), lambda i, j, te, tv: (te[i], 0, jsel(i, j, te, tv))),
            pl.BlockSpec((1, D_MODEL, tf), lambda i, j, te, tv: (te[i], 0, jsel(i, j, te, tv))),
            pl.BlockSpec((1, tf, D_MODEL), lambda i, j, te, tv: (te[i], jsel(i, j, te, tv), 0)),
        ],
        out_specs=pl.BlockSpec((tme, D_MODEL), lambda i, j, te, tv: (i, 0)),
        scratch_shapes=[pltpu.VMEM((2, tme, D_MODEL), F32), pltpu.VMEM((tme, D_MODEL), BF16),
                        pltpu.VMEM((tme, D_MODEL), F32), pltpu.SemaphoreType.DMA((2,))],
    )
    return pl.pallas_call(
        functools.partial(_moe_kernel, tme=tme, nj=nj),
        grid_spec=grid_spec,
        out_shape=jax.ShapeDtypeStruct((n_tiles * tme, D_MODEL), F32),
        compiler_params=pltpu.CompilerParams(dimension_semantics=("arbitrary", "arbitrary"), vmem_limit_bytes=VMEM_LIMIT),
        name="moe_experts",
    )(tile_expert, tile_valid, src_tok3, src_tok3, n_all, wg, wu, wd)


def _route_plan(route_idx, tme, n_tiles):
    n_tok = route_idx.shape[0]
    e_flat = route_idx.reshape(-1)
    onehot = (e_flat[:, None] == jnp.arange(N_EXPERTS, dtype=I32)[None, :]).astype(I32)
    csum = jnp.cumsum(onehot, axis=0)
    rank = jnp.sum(csum * onehot, axis=1) - 1
    cnt = csum[-1]
    tiles_e = (cnt + tme - 1) // tme
    tile_end = jnp.cumsum(tiles_e)
    row_off = (tile_end - tiles_e) * tme
    dst = jnp.sum(onehot * row_off[None, :], axis=1) + rank
    tile_id = jnp.arange(n_tiles, dtype=I32)
    tile_expert = jnp.minimum(jnp.sum((tile_id[:, None] >= tile_end[None, :]).astype(I32), axis=1), N_EXPERTS - 1)
    tile_valid = (tile_id < tile_end[-1]).astype(I32)
    tok_of = jnp.arange(2 * n_tok, dtype=I32) // 2
    src_tok = jnp.zeros((n_tiles * tme,), I32).at[dst].set(tok_of)
    return dst.reshape(n_tok, 2), src_tok.reshape(n_tiles, 1, tme), tile_expert.astype(I32), tile_valid


def _moe_combine_kernel(pos_ref, y_hbm, x1_ref, rw_ref, g_ref, o_ref, y_s, sem, *, tm, final_norm):
    def issue(r, c):
        p0 = pos_ref[0, 0, 2 * r]
        p1 = pos_ref[0, 0, 2 * r + 1]
        pltpu.make_async_copy(y_hbm.at[pl.ds(p0, 1), :], y_s.at[0, pl.ds(r, 1), :], sem).start()
        pltpu.make_async_copy(y_hbm.at[pl.ds(p1, 1), :], y_s.at[1, pl.ds(r, 1), :], sem).start()
        return c
    lax.fori_loop(0, tm, issue, 0, unroll=8)
    for slot in range(2):
        pltpu.make_async_copy(y_hbm.at[pl.ds(0, tm), :], y_s.at[slot], sem).wait()
    rw = rw_ref[...]
    out = x1_ref[...] + rw[:, 0:1] * y_s[0] + rw[:, 1:2] * y_s[1]
    if final_norm:
        out = _rmsnorm(out, g_ref[...])
    o_ref[...] = out


def _moe_combine_call(pos3, y_sorted, x1, rw, g_final, tm, final_norm):
    m = x1.shape[0]
    row = lambda i: (i, 0)
    return pl.pallas_call(
        functools.partial(_moe_combine_kernel, tm=tm, final_norm=final_norm),
        grid=(m // tm,),
        in_specs=[
            pl.BlockSpec((1, 1, 2 * tm), lambda i: (i, 0, 0), memory_space=pltpu.SMEM),
            pl.BlockSpec(memory_space=pl.ANY),
            pl.BlockSpec((tm, D_MODEL), row),
            pl.BlockSpec((tm, LANES), row),
            pl.BlockSpec((1, D_MODEL), lambda i: (0, 0)),
        ],
        out_specs=pl.BlockSpec((tm, D_MODEL), row),
        out_shape=jax.ShapeDtypeStruct((m, D_MODEL), F32),
        scratch_shapes=[pltpu.VMEM((2, tm, D_MODEL), F32), pltpu.SemaphoreType.DMA],
        compiler_params=pltpu.CompilerParams(dimension_semantics=("arbitrary",), vmem_limit_bytes=VMEM_LIMIT),
        name="moe_combine",
    )(pos3, y_sorted, x1, rw, g_final)


def _final_norm_kernel(x_ref, g_ref, o_ref):
    o_ref[...] = _rmsnorm(x_ref[...], g_ref[...])


def _final_norm_call(x, g, tm):
    m = x.shape[0]
    row = lambda i: (i, 0)
    return pl.pallas_call(
        _final_norm_kernel,
        grid=(m // tm,),
        in_specs=[pl.BlockSpec((tm, D_MODEL), row), pl.BlockSpec((1, D_MODEL), lambda i: (0, 0))],
        out_specs=pl.BlockSpec((tm, D_MODEL), row),
        out_shape=jax.ShapeDtypeStruct((m, D_MODEL), F32),
        compiler_params=pltpu.CompilerParams(dimension_semantics=("arbitrary",)),
        name="final_norm",
    )(x, g)


def _rope_tables(pos):
    half = HEAD_DIM // 2
    inv_freq = np.power(np.float64(ROPE_THETA), -np.arange(half, dtype=np.float64) * (2.0 / HEAD_DIM))
    ang = np.asarray(pos, np.float64)[:, None] * inv_freq[None, :]
    c, s = np.cos(ang), np.sin(ang)
    cos_t = np.concatenate([c, c, c, c], axis=1).astype(np.float32)
    sin_t = np.concatenate([-s, s, -s, s], axis=1).astype(np.float32)
    return jnp.asarray(cos_t), jnp.asarray(sin_t)


def _pack_w_in(w):
    gates0 = QKV_W + GROUP_W
    pad = jnp.zeros((w.shape[0], LANES - 2 * N_HEADS), w.dtype)
    return jnp.concatenate([w[:, :gates0], w[:, gates0 + 2 * N_HEADS:], w[:, gates0:gates0 + 2 * N_HEADS], pad],
                           axis=1)


def _lane_row(vals, offset):
    return jnp.zeros((1, LANES), F32).at[0, offset:offset + vals.shape[0]].set(vals.astype(F32))


def _pad_rows(x, rows, front=0):
    return jnp.pad(x, ((0, 0), (front, rows - front - x.shape[1]), (0, 0)))


def kernel(x_prompt, x_sample, state_conv, state_delta, cache_win_k, cache_win_v, norm_mix, w_in, conv_w, a_log,
           dt_bias, gdn_norm, w_out, norm_ffn, ffn_gate, ffn_up, ffn_down, router, moe_gate, moe_up, moe_down,
           final_norm):
    bp, seq, _ = x_prompt.shape
    bs, t_len, _ = x_sample.shape
    depth = w_in.shape[0]
    n_p, n_s = bp * seq, bs * t_len
    n_all = n_p + n_s
    tm_p, tm_s = 512, n_s
    gdn_rows = 256
    n_buf = cache_win_k.shape[2]
    n_keep = min(WINDOW_MAX, seq)

    cos_p, sin_p = _rope_tables(np.arange(seq))
    cos_s, sin_s = _rope_tables(PAST_LEN + (np.arange(n_s) % t_len))
    xp = x_prompt.reshape(n_p, D_MODEL)
    xs = x_sample.reshape(n_s, D_MODEL)
    g_final = final_norm.reshape(1, D_MODEL)

    conv_p, delta_p, wk_p, wv_p = [], [], [], []
    conv_s, delta_s = [], []
    new_k_all = new_v_all = None
    cache_kt = cache_win_k.astype(F32).transpose(0, 1, 3, 4, 2)
    cache_vt = cache_win_v.astype(F32).transpose(0, 1, 3, 4, 2)
    for l in range(depth):
        w_packed_f = _pack_w_in(w_in[l].astype(F32))
        w_packed = w_packed_f.astype(BF16)
        w_out_f = w_out[l].astype(F32)
        g_mix = norm_mix[l].reshape(1, D_MODEL)
        alog_row = _lane_row(a_log[l], N_HEADS)
        dtb_row = _lane_row(dt_bias[l], N_HEADS)
        gn_row = jnp.tile(gdn_norm[l].astype(F32), N_HEADS).reshape(1, GROUP_W)
        w_out_b = w_out[l].astype(BF16)
        g_ffn = norm_ffn[l].reshape(1, D_MODEL)
        is_moe = l % 2 == 1
        idx = l // 2

        dils = tuple(d for _, d in DILATIONS)
        qkv, z, gt, k, v, *qkv_d = _inproj_call(xp, g_mix, w_packed, cos_p, sin_p, tm_p, dils=dils)
        oa, s_bd = _gdn_prompt_call(qkv, z, gt, conv_w[l], alog_row, dtb_row, gn_row, bp, seq, gdn_rows)
        branches = [_swa_call(*qkv_d[3 * i:3 * i + 3], dil, bp, seq) for i, dil in enumerate(dils)]
        conv_p.append(qkv.reshape(bp, seq, QKV_W)[:, seq - (CONV_TAPS - 1):])
        delta_p.append(_state_from_block_diag(s_bd))
        wk_p.append(k.reshape(bp, seq, GROUP_W)[:, seq - n_keep:].reshape(bp, n_keep, N_HEADS, HEAD_DIM))
        wv_p.append(v.reshape(bp, seq, GROUP_W)[:, seq - n_keep:].reshape(bp, n_keep, N_HEADS, HEAD_DIM))

        qkv_s, z_s, gt_s, q_s, k_s, v_s = _inproj_call(xs, g_mix, w_packed_f, cos_s, sin_s, tm_s, precise=True)
        prev_pad = _pad_rows(state_conv[l].astype(F32), 8, front=8 - (CONV_TAPS - 1))
        oa_s_pad, s_bd_s = _gdn_sample_call(
            _pad_rows(qkv_s.reshape(bs, t_len, QKV_W), CHUNK), prev_pad,
            _pad_rows(z_s.reshape(bs, t_len, GROUP_W), CHUNK), _pad_rows(gt_s.reshape(bs, t_len, LANES), CHUNK),
            conv_w[l], alog_row, dtb_row, gn_row, _state_to_block_diag(state_delta[l]), t_len, min(8, bs))
        oa_s = oa_s_pad[:, :t_len].reshape(n_s, GROUP_W)
        heads = (bs, t_len, N_HEADS, HEAD_DIM)
        new_k_all, new_v_all, o_s3, l_s3 = _swa_sample_call(
            q_s.reshape(heads), k_s.reshape(heads), v_s.reshape(heads), cache_kt, cache_vt, l, new_k_all, new_v_all)
        conv_all = jnp.concatenate([state_conv[l].astype(F32), qkv_s.reshape(bs, t_len, QKV_W)], axis=1)
        conv_s.append(conv_all[:, conv_all.shape[1] - (CONV_TAPS - 1):])
        delta_s.append(_state_from_block_diag(s_bd_s))
        nd = len(DILATIONS)
        o_br_s = [o_s3[:, :, i * T_PAD:i * T_PAD + t_len].transpose(0, 2, 1, 3).reshape(n_s, GROUP_W)
                  for i in range(nd)]
        l_br_s = [jnp.pad(l_s3[:, :, i * T_PAD:i * T_PAD + t_len, 0].transpose(0, 2, 1).reshape(n_s, N_HEADS),
                          ((0, 0), (0, LANES - N_HEADS))) for i in range(nd)]

        o_br_p = [b[0] for b in branches]
        l_br_p = [b[1] for b in branches]
        if not is_moe:
            x1p, np_ = _outproj_call(oa.reshape(n_p, GROUP_W), o_br_p, l_br_p, xp, w_out_b, g_ffn, None, tm_p, BF16,
                                     dils=dils)
            x1s, ns_ = _outproj_call(oa_s, o_br_s, l_br_s, xs, w_out_f, g_ffn, None, tm_s, F32, precise=True)
            wg, wu, wd = ffn_gate[idx].astype(BF16), ffn_up[idx].astype(BF16), ffn_down[idx].astype(BF16)
            xp = _ffn_call(np_, x1p, wg, wu, wd, tm_p)
            xs = _ffn_stream_call(ns_, x1s, ffn_gate[idx].astype(F32), ffn_up[idx].astype(F32),
                                  ffn_down[idx].astype(F32), 256, precise=True)
            if l == depth - 1:
                xp = _final_norm_call(xp, g_final, tm_p)
                xs = _final_norm_call(xs, g_final, tm_s)
        else:
            router_pad = jnp.pad(router[idx].astype(F32), ((0, 0), (0, LANES - N_EXPERTS)))
            x1p, n_tok_p, ri_p, rw_p = _outproj_call(
                oa.reshape(n_p, GROUP_W), o_br_p, l_br_p, xp, w_out_b, g_ffn, router_pad, tm_p, F32, dils=dils)
            x1s, n_tok_s, ri_s, rw_s = _outproj_call(oa_s, o_br_s, l_br_s, xs, w_out_f, g_ffn, router_pad, tm_s, F32,
                                                     precise=True)
            n_tok = jnp.concatenate([n_tok_p, n_tok_s], axis=0)
            tme, tf = 1120, 896
            n_tiles = (2 * n_all + N_EXPERTS * (tme - 1)) // tme
            ridx = jnp.concatenate([ri_p[:, :2], ri_s[:, :2]], axis=0)
            dst, src_tok3, tile_expert, tile_valid = _route_plan(ridx, tme, n_tiles)
            y_sorted = _moe_call(n_tok, tile_expert, tile_valid, src_tok3, moe_gate[idx], moe_up[idx], moe_down[idx],
                                 tme, tf)
            last = l == depth - 1
            tmc_p, tmc_s = 256, n_s
            xp = _moe_combine_call(dst[:n_p].reshape(n_p // tmc_p, 1, 2 * tmc_p), y_sorted, x1p, rw_p, g_final,
                                   tmc_p, last)
            xs = _moe_combine_call(dst[n_p:].reshape(n_s // tmc_s, 1, 2 * tmc_s), y_sorted, x1s, rw_s, g_final,
                                   tmc_s, last)

    y_prompt = xp.reshape(bp, seq, D_MODEL)
    y_sample = xs.reshape(bs, t_len, D_MODEL)
    return (y_prompt, y_sample, jnp.stack(conv_p), jnp.stack(delta_p), jnp.stack(wk_p), jnp.stack(wv_p),
            jnp.stack(conv_s), jnp.stack(delta_s), new_k_all.transpose(0, 1, 4, 2, 3),
            new_v_all.transpose(0, 1, 4, 2, 3))
```

```python
import functools

import numpy as np
import jax
import jax.numpy as jnp
from jax import lax
from jax.experimental import pallas as pl
from jax.experimental.pallas import tpu as pltpu

F32 = jnp.float32
BF16 = jnp.bfloat16
I32 = jnp.int32

EPS = 1e-6
NEG_BIG = -1e30
D_MODEL = 1024
HEAD_DIM = 64
N_HEADS = 8
GROUP_W = N_HEADS * HEAD_DIM
QKV_W = 3 * GROUP_W
CONV_TAPS = 4
CHUNK = 64
HEADS_PER_PACK = 4
PACK_W = HEADS_PER_PACK * HEAD_DIM
DILATIONS = ((128, 1), (512, 4), (2048, 16))
BAND = 128
WINDOW_MAX = 2048
ROPE_THETA = 10000.0
PAST_LEN = 16384
N_EXPERTS = 8
PROJ_PACKED_W = QKV_W + 4 * GROUP_W + 128
LANES = 128
VMEM_LIMIT = 56 * 1024 * 1024


def _dot(a, b):
    return jnp.dot(a, b, preferred_element_type=F32)


def _dot_nt(a, b):
    return lax.dot_general(a, b, (((1,), (1,)), ((), ())), preferred_element_type=F32)


def _mm(a, b, precise=False):
    if precise:
        return jnp.dot(a.astype(F32), b.astype(F32), preferred_element_type=F32, precision=lax.Precision.HIGHEST)
    return jnp.dot(a.astype(BF16), b.astype(BF16), preferred_element_type=F32)


def _split2(x):
    hi = x.astype(BF16)
    lo = (x - hi.astype(F32)).astype(BF16)
    return hi, lo


def _split3(x):
    hi = x.astype(BF16)
    r = x - hi.astype(F32)
    mid = r.astype(BF16)
    lo = (r - mid.astype(F32)).astype(BF16)
    return hi, mid, lo


def _dot_x3(a, b):
    ah, al = _split2(a)
    bh, bl = _split2(b)
    m = a.shape[0]
    r = _dot(jnp.concatenate([ah, al], axis=0), bh)
    return r[:m] + r[m:] + _dot(ah, bl)


def _dot_exact_rhs01(x, e):
    return sum(_dot(p, e) for p in _split3(x))


def _sigmoid(x):
    return 1.0 / (1.0 + jnp.exp(-x))


def _silu(x):
    return x * _sigmoid(x)


def _softplus(x):
    return jnp.maximum(x, 0.0) + jnp.log1p(jnp.exp(-jnp.abs(x)))


def _rmsnorm(x, g):
    ms = jnp.mean(x * x, axis=-1, keepdims=True)
    return x * lax.rsqrt(ms + EPS) * g


def _head_ones(width):
    r = lax.broadcasted_iota(I32, (width, width), 0) // HEAD_DIM
    c = lax.broadcasted_iota(I32, (width, width), 1) // HEAD_DIM
    return jnp.where(r == c, 1.0, 0.0).astype(BF16)


def _head_sum(x, ones_bd, precise=True):
    if not precise:
        return _dot(x.astype(BF16), ones_bd)
    hi, lo = _split2(x)
    return _dot(hi, ones_bd) + _dot(lo, ones_bd)


def _rope(x, cos, sin):
    n = x.shape[1]
    reps = n // LANES
    c = jnp.concatenate([cos] * reps, axis=1)
    s = jnp.concatenate([sin] * reps, axis=1)
    lane = lax.broadcasted_iota(I32, x.shape, 1)
    first_half = (lane & (HEAD_DIM - 1)) < HEAD_DIM // 2
    partner = jnp.where(first_half, pltpu.roll(x, n - HEAD_DIM // 2, 1), pltpu.roll(x, HEAD_DIM // 2, 1))
    return x * c + partner * s


def _residue_perm(tm, dil):
    i = np.arange(tm)
    src = (i % (tm // dil)) * dil + i // (tm // dil)
    p = np.zeros((tm, tm), np.float32)
    p[i, src] = 1.0
    return p


def _inproj_kernel(x_ref, g_ref, w_ref, cos_ref, sin_ref, *rest, precise, dils):
    y = _rmsnorm(x_ref[...], g_ref[...])
    if not precise:
        y = y.astype(BF16)

    def mm(lo, hi):
        return _mm(y, w_ref[:, lo:hi], precise)

    cos = cos_ref[...]
    sin = sin_ref[...]
    n_perm = 0 if dils is None else sum(1 for d in dils if d > 1)
    perm_refs, outs = rest[:n_perm], rest[n_perm:]
    o = 0
    outs[0][...] = mm(o, o + QKV_W)
    o += QKV_W
    outs[1][...] = mm(o, o + GROUP_W)
    o += GROUP_W
    q = _rope(mm(o, o + GROUP_W), cos, sin)
    o += GROUP_W
    k = _rope(mm(o, o + GROUP_W), cos, sin)
    o += GROUP_W
    v = mm(o, o + GROUP_W)
    o += GROUP_W
    outs[2][...] = mm(o, o + LANES)
    if dils is None:
        outs[3][...] = q
        outs[4][...] = k
        outs[5][...] = v
        return
    outs[3][...] = k
    outs[4][...] = v
    tm = q.shape[0]
    qkv_b = [t.astype(BF16) for t in (q, k, v)]
    nxt, pi = 5, 0
    for d in dils:
        if d == 1:
            for t in qkv_b:
                outs[nxt][...] = t
                nxt += 1
            continue
        sub = tm // d
        for t in qkv_b:
            tp = _dot(perm_refs[pi][...], t).astype(BF16)
            for r in range(d):
                outs[nxt][:, r * GROUP_W:(r + 1) * GROUP_W] = tp[r * sub:(r + 1) * sub, :]
            nxt += 1
        pi += 1


def _inproj_call(x, g, w_packed, cos_tab, sin_tab, tm, precise=False, dils=None):
    m = x.shape[0]
    n_tab = cos_tab.shape[0] // tm
    row = lambda i: (i, 0)
    const = lambda i: (0, 0)
    tab = lambda i: (i % n_tab, 0)
    in_specs = [
        pl.BlockSpec((tm, D_MODEL), row),
        pl.BlockSpec((1, D_MODEL), const),
        pl.BlockSpec((D_MODEL, PROJ_PACKED_W), const),
        pl.BlockSpec((tm, LANES), tab),
        pl.BlockSpec((tm, LANES), tab),
    ]
    args = [x, g, w_packed, cos_tab, sin_tab]
    outs = [(m, QKV_W, tm, F32), (m, GROUP_W, tm, F32), (m, LANES, tm, F32)]
    if dils is None:
        outs += [(m, GROUP_W, tm, F32)] * 3
    else:
        outs += [(m, GROUP_W, tm, F32)] * 2
        for d in dils:
            outs += [(m // d, d * GROUP_W, tm // d, BF16)] * 3
            if d > 1:
                in_specs.append(pl.BlockSpec((tm, tm), const))
                args.append(jnp.asarray(_residue_perm(tm, d), BF16))
    return pl.pallas_call(
        functools.partial(_inproj_kernel, precise=precise, dils=dils),
        grid=(m // tm,),
        in_specs=in_specs,
        out_specs=[pl.BlockSpec((rows, w), row) for _, w, rows, _ in outs],
        out_shape=[jax.ShapeDtypeStruct((n, w), dt) for n, w, _, dt in outs],
        compiler_params=pltpu.CompilerParams(dimension_semantics=("arbitrary",), vmem_limit_bytes=VMEM_LIMIT),
        name="inproj",
    )(*args)


def _pack_masks():
    lane = lax.broadcasted_iota(I32, (1, PACK_W), 1)
    return [lane // HEAD_DIM == h for h in range(HEADS_PER_PACK)]


def _gdn_chunk_consts():
    i = lax.broadcasted_iota(I32, (CHUNK, PACK_W), 0)
    j = lax.broadcasted_iota(I32, (CHUNK, PACK_W), 1) & (HEAD_DIM - 1)
    r = lax.broadcasted_iota(I32, (CHUNK, CHUNK), 0)
    c = lax.broadcasted_iota(I32, (CHUNK, CHUNK), 1)
    br = lax.broadcasted_iota(I32, (PACK_W, PACK_W), 0) // HEAD_DIM
    bc = lax.broadcasted_iota(I32, (PACK_W, PACK_W), 1) // HEAD_DIM
    return dict(
        head_masks=_pack_masks(),
        lower=i >= j,
        strict=i > j,
        upper_f=jnp.where(i <= j, 1.0, 0.0),
        eye=jnp.where(i == j, 1.0, 0.0),
        tri=jnp.where(r >= c, 1.0, 0.0).astype(BF16),
        block=br == bc,
    )


def _bmm_dn(a, b, dn, precise):
    if not precise:
        return lax.dot_general(a.astype(BF16), b.astype(BF16), dn, preferred_element_type=F32)
    ah, al = _split2(a.astype(F32))
    bh, bl = _split2(b.astype(F32))
    m = a.shape[1]
    r = lax.dot_general(jnp.concatenate([ah, al], axis=1), bh, dn, preferred_element_type=F32)
    return r[:, :m] + r[:, m:] + lax.dot_general(ah, bl, dn, preferred_element_type=F32)


def _bmm(a, b, precise=False):
    return _bmm_dn(a, b, (((2,), (1,)), ((0,), (0,))), precise)


def _bmm_nt(a, b, precise=False):
    return _bmm_dn(a, b, (((2,), (2,)), ((0,), (0,))), precise)


def _block_diag_b(x, head_masks):
    return jnp.concatenate([jnp.where(m, x, 0.0) for m in head_masks], axis=1)


def _gdn_local(qn, kn, v, bw, gw, cst, precise):
    hm = cst["head_masks"]
    nb = qn.shape[0]
    tri = jnp.broadcast_to(cst["tri"], (nb, CHUNK, CHUNK))
    gcum = sum(_bmm(tri, p) for p in _split3(gw))
    grow = jnp.sum(gw * cst["upper_f"], axis=1, keepdims=True)
    diff = gcum - grow
    decay = jnp.where(cst["lower"], jnp.exp(jnp.where(cst["lower"], diff, 0.0)), 0.0)
    kq = _bmm_nt(jnp.concatenate([kn, qn], axis=1), _block_diag_b(kn, hm), precise)
    kk, qk = kq[:, :CHUNK], kq[:, CHUNK:]
    nmat = bw * kk * jnp.where(cst["strict"], decay, 0.0)
    p = -nmat
    y = cst["eye"] + p
    q = _bmm(p, _block_diag_b(p, hm), precise)
    n_sq = CHUNK.bit_length() - 1
    for s in range(1, n_sq):
        q_bd = _block_diag_b(q, hm)
        if s < n_sq - 1:
            both = _bmm(jnp.concatenate([y, q], axis=1), q_bd, precise)
            y = y + both[:, :CHUNK]
            q = both[:, CHUNK:]
        else:
            y = y + _bmm(y, q_bd, precise)
    egc = jnp.exp(gcum)
    rhs = jnp.concatenate([_block_diag_b(bw * v, hm), _block_diag_b(bw * egc * kn, hm)], axis=2)
    sol = _bmm(y, rhs, precise)
    glast = gcum[:, CHUNK - 1:CHUNK, :]
    return (sol[:, :, :PACK_W], sol[:, :, PACK_W:], qk * decay, qn * egc, kn * jnp.exp(glast - gcum),
            jnp.exp(glast))


def _gdn_scan_step(u, wk, aqk, qd, kd, gl, state, cst, precise):
    hm = cst["head_masks"]
    m1 = _bmm(jnp.concatenate([wk, qd], axis=1), state, precise)
    w = u - m1[:, :CHUNK]
    o = m1[:, CHUNK:] + _bmm(aqk, _block_diag_b(w, hm), precise)
    inc = _bmm(jnp.swapaxes(kd, 1, 2), w, precise)
    return o, state * gl + jnp.where(cst["block"], inc, 0.0)


def _gdn_kernel(*refs, nb, rows, carry, n_valid, precise):
    if carry:
        (qkv_ref, prev_ref, z_ref, gt_ref, cw_ref, al_ref, db_ref, gn_ref, o_ref, sout_ref, xp_s, st_s) = refs
        s0_ref = None
    else:
        (qkv_ref, prev_ref, z_ref, gt_ref, cw_ref, al_ref, db_ref, gn_ref, s0_ref, o_ref, sout_ref, xp_s, st_s) = refs
    step = pl.program_id(0)
    ones_bd = _head_ones(GROUP_W)
    lane = lax.broadcasted_iota(I32, (1, LANES), 1)
    er = lax.broadcasted_iota(I32, (LANES, 2 * GROUP_W), 0)
    ec = lax.broadcasted_iota(I32, (LANES, 2 * GROUP_W), 1)
    expand = jnp.where(((ec < GROUP_W) & (er == ec // HEAD_DIM))
                       | ((ec >= GROUP_W) & (er == N_HEADS + (ec - GROUP_W) // HEAD_DIM)), 1.0, 0.0).astype(BF16)
    cw = cw_ref[...]

    if carry:
        @pl.when(step == 0)
        def _():
            st_s[...] = jnp.zeros_like(st_s)
    else:
        st_s[...] = s0_ref[...].reshape(st_s.shape)

    per_seq = []
    for b in range(nb):
        prev = prev_ref[b]
        if carry:
            prev = jnp.where(step == 0, 0.0, prev)
        xp_s[b, 0:8, :] = prev
        xp_s[b, 8:8 + rows, :] = qkv_ref[b]
        acc = cw[CONV_TAPS - 1:CONV_TAPS, :] * xp_s[b, 8:8 + rows, :]
        for t in range(1, CONV_TAPS):
            acc = acc + cw[CONV_TAPS - 1 - t:CONV_TAPS - t, :] * xp_s[b, 8 - t:8 - t + rows, :]
        cq = _silu(acc)
        q = cq[:, :GROUP_W]
        k = cq[:, GROUP_W:2 * GROUP_W]
        v = cq[:, 2 * GROUP_W:]
        qn = q * lax.rsqrt(_head_sum(q * q, ones_bd, precise) + EPS) * (HEAD_DIM ** -0.5)
        kn = k * lax.rsqrt(_head_sum(k * k, ones_bd, precise) + EPS)
        t_raw = gt_ref[b]
        beta = _sigmoid(t_raw)
        gdec = -jnp.exp(al_ref[...]) * _softplus(t_raw + db_ref[...])
        wide = _dot_exact_rhs01(jnp.where(lane < N_HEADS, beta, gdec), expand)
        bw = wide[:, :GROUP_W]
        gw = wide[:, GROUP_W:]
        if n_valid < rows:
            live = lax.broadcasted_iota(I32, (rows, 1), 0) < n_valid
            bw = jnp.where(live, bw, 0.0)
            gw = jnp.where(live, gw, 0.0)
            qn = jnp.where(live, qn, 0.0)
            kn = jnp.where(live, kn, 0.0)
            v = jnp.where(live, v, 0.0)
        per_seq.append((qn, kn, v, bw, gw))

    cst = _gdn_chunk_consts()
    n_chunks = rows // CHUNK
    n_pack = N_HEADS // HEADS_PER_PACK

    def instances(idx):
        return jnp.stack([per_seq[b][idx][c * CHUNK:(c + 1) * CHUNK, g * PACK_W:(g + 1) * PACK_W]
                          for c in range(n_chunks) for b in range(nb) for g in range(n_pack)], axis=0)

    u, wk, aqk, qd, kd, gl = _gdn_local(*(instances(i) for i in range(5)), cst, precise)
    n_chain = nb * n_pack
    state = st_s[...]
    o_chunks = []
    for c in range(n_chunks):
        sl = slice(c * n_chain, (c + 1) * n_chain)
        o_c, state = _gdn_scan_step(u[sl], wk[sl], aqk[sl], qd[sl], kd[sl], gl[sl], state, cst, precise)
        o_chunks.append(o_c)
    st_s[...] = state

    for b in range(nb):
        o = jnp.concatenate(
            [jnp.concatenate([o_chunks[c][b * n_pack + g] for g in range(n_pack)], axis=1) for c in range(n_chunks)],
            axis=0)
        ms = _head_sum(o * o, ones_bd, precise) * (1.0 / HEAD_DIM)
        o_ref[b] = (o * lax.rsqrt(ms + EPS) * gn_ref[...] * _silu(z_ref[b])).astype(o_ref.dtype)

    if carry:
        @pl.when(step == pl.num_programs(0) - 1)
        def _():
            sout_ref[...] = state.reshape(sout_ref.shape)
    else:
        sout_ref[...] = state.reshape(sout_ref.shape)


def _gdn_scratch(nb, rows):
    return [pltpu.VMEM((nb, rows + 8, QKV_W), F32),
            pltpu.VMEM((nb * (N_HEADS // HEADS_PER_PACK), PACK_W, PACK_W), F32)]


def _gdn_prompt_call(qkv, z, gt, conv_w, alog_row, dtb_row, gn_row, batch, seq, rows):
    qkv3 = qkv.reshape(batch, seq, QKV_W)
    z3 = z.reshape(batch, seq, GROUP_W)
    gt3 = gt.reshape(batch, seq, LANES)
    blk = lambda i: (0, i, 0)
    const2 = lambda i: (0, 0)
    n_pack = N_HEADS // HEADS_PER_PACK
    return pl.pallas_call(
        functools.partial(_gdn_kernel, nb=batch, rows=rows, carry=True, n_valid=rows, precise=False),
        grid=(seq // rows,),
        in_specs=[
            pl.BlockSpec((batch, rows, QKV_W), blk),
            pl.BlockSpec((batch, 8, QKV_W), lambda i: (0, jnp.maximum(i * (rows // 8) - 1, 0), 0)),
            pl.BlockSpec((batch, rows, GROUP_W), blk),
            pl.BlockSpec((batch, rows, LANES), blk),
            pl.BlockSpec((CONV_TAPS, QKV_W), const2),
            pl.BlockSpec((1, LANES), const2),
            pl.BlockSpec((1, LANES), const2),
            pl.BlockSpec((1, GROUP_W), const2),
        ],
        out_specs=[
            pl.BlockSpec((batch, rows, GROUP_W), blk),
            pl.BlockSpec((batch, n_pack, PACK_W, PACK_W), lambda i: (0, 0, 0, 0)),
        ],
        out_shape=[
            jax.ShapeDtypeStruct((batch, seq, GROUP_W), BF16),
            jax.ShapeDtypeStruct((batch, n_pack, PACK_W, PACK_W), F32),
        ],
        scratch_shapes=_gdn_scratch(batch, rows),
        compiler_params=pltpu.CompilerParams(dimension_semantics=("arbitrary",), vmem_limit_bytes=VMEM_LIMIT),
        name="gdn_prompt",
    )(qkv3, qkv3, z3, gt3, conv_w, alog_row, dtb_row, gn_row)


def _gdn_sample_call(qkv_pad, prev_pad, z_pad, gt_pad, conv_w, alog_row, dtb_row, gn_row, s0_bd, n_valid, nb):
    n_seq = qkv_pad.shape[0]
    blk = lambda i: (i, 0, 0)
    blk4 = lambda i: (i, 0, 0, 0)
    const2 = lambda i: (0, 0)
    n_pack = N_HEADS // HEADS_PER_PACK
    return pl.pallas_call(
        functools.partial(_gdn_kernel, nb=nb, rows=CHUNK, carry=False, n_valid=n_valid, precise=True),
        grid=(n_seq // nb,),
        in_specs=[
            pl.BlockSpec((nb, CHUNK, QKV_W), blk),
            pl.BlockSpec((nb, 8, QKV_W), blk),
            pl.BlockSpec((nb, CHUNK, GROUP_W), blk),
            pl.BlockSpec((nb, CHUNK, LANES), blk),
            pl.BlockSpec((CONV_TAPS, QKV_W), const2),
            pl.BlockSpec((1, LANES), const2),
            pl.BlockSpec((1, LANES), const2),
            pl.BlockSpec((1, GROUP_W), const2),
            pl.BlockSpec((nb, n_pack, PACK_W, PACK_W), blk4),
        ],
        out_specs=[
            pl.BlockSpec((nb, CHUNK, GROUP_W), blk),
            pl.BlockSpec((nb, n_pack, PACK_W, PACK_W), blk4),
        ],
        out_shape=[
            jax.ShapeDtypeStruct((n_seq, CHUNK, GROUP_W), F32),
            jax.ShapeDtypeStruct((n_seq, n_pack, PACK_W, PACK_W), F32),
        ],
        scratch_shapes=_gdn_scratch(nb, CHUNK),
        compiler_params=pltpu.CompilerParams(dimension_semantics=("arbitrary",), vmem_limit_bytes=VMEM_LIMIT),
        name="gdn_sample",
    )(qkv_pad, prev_pad, z_pad, gt_pad, conv_w, alog_row, dtb_row, gn_row, s0_bd)


def _state_to_block_diag(s):
    n = s.shape[0]
    n_pack = N_HEADS // HEADS_PER_PACK
    s5 = s.reshape(n, n_pack, HEADS_PER_PACK, HEAD_DIM, HEAD_DIM).astype(F32)
    eye = jnp.eye(HEADS_PER_PACK, dtype=F32)
    return jnp.einsum("bghde,hk->bghdke", s5, eye).reshape(n, n_pack, PACK_W, PACK_W)


def _state_from_block_diag(s_bd):
    n = s_bd.shape[0]
    n_pack = N_HEADS // HEADS_PER_PACK
    s6 = s_bd.reshape(n, n_pack, HEADS_PER_PACK, HEAD_DIM, HEADS_PER_PACK, HEAD_DIM)
    diag = jnp.stack([s6[:, :, h, :, h, :] for h in range(HEADS_PER_PACK)], axis=2)
    return diag.reshape(n, N_HEADS, HEAD_DIM, HEAD_DIM)


def _swa_kernel(q_ref, kc_ref, kp_ref, vc_ref, vp_ref, o_ref, l_ref, *, tq):
    first_block = pl.program_id(2) == 0
    n_sub = tq // BAND
    lane = lax.broadcasted_iota(I32, (1, LANES), 1)
    low = lane < HEAD_DIM
    qi = lax.broadcasted_iota(I32, (BAND, 2 * BAND), 0)
    kj = lax.broadcasted_iota(I32, (BAND, 2 * BAND), 1)
    dist = qi + BAND - kj
    band = (dist >= 0) & (dist <= BAND)
    band_first = band & ((kj >= BAND) | jnp.logical_not(first_block))
    lane_l = lax.broadcasted_iota(I32, (BAND, LANES), 1)
    lse_acc = [jnp.zeros((BAND, LANES), F32) for _ in range(n_sub)]
    for p in range(GROUP_W // LANES):
        ls = slice(p * LANES, (p + 1) * LANES)
        q_pair = (q_ref[:, ls].astype(F32) * (HEAD_DIM ** -0.5)).astype(BF16)
        k_all = jnp.concatenate([kp_ref[:, ls], kc_ref[:, ls]], axis=0)
        v_all = jnp.concatenate([vp_ref[:, ls], vc_ref[:, ls]], axis=0)
        o_sub = [jnp.zeros((BAND, LANES), F32) for _ in range(n_sub)]
        for hh in range(2):
            hmask = low if hh == 0 else jnp.logical_not(low)
            k_h = jnp.where(hmask, k_all, jnp.zeros_like(k_all))
            v_h = jnp.where(hmask, v_all, jnp.zeros_like(v_all))
            head = 2 * p + hh
            for s in range(n_sub):
                sc = _dot_nt(q_pair[s * BAND:(s + 1) * BAND], k_h[s * BAND:s * BAND + 2 * BAND])
                valid = band_first if s == 0 else band
                sc = jnp.where(valid, sc, NEG_BIG)
                m = jnp.max(sc, axis=1, keepdims=True)
                pm = jnp.exp(sc - m)
                den = jnp.sum(pm, axis=1, keepdims=True)
                num = _dot(pm.astype(BF16), v_h[s * BAND:s * BAND + 2 * BAND])
                o_sub[s] = o_sub[s] + num / den
                lse_acc[s] = lse_acc[s] + jnp.where(lane_l == head, m + jnp.log(den), 0.0)
        for s in range(n_sub):
            o_ref[s * BAND:(s + 1) * BAND, ls] = o_sub[s].astype(o_ref.dtype)
    for s in range(n_sub):
        l_ref[s * BAND:(s + 1) * BAND, :] = lse_acc[s]


def _swa_call(qv, kv, vv, dil, batch, seq):
    sub_len = seq // dil
    tq = min(512, sub_len)
    nq = sub_len // tq
    rows = batch * seq // dil
    cur = lambda b, r, i: (b * nq + i, r)
    prv = lambda b, r, i: (jnp.maximum((b * nq + i) * (tq // BAND) - 1, 0), r)
    return pl.pallas_call(
        functools.partial(_swa_kernel, tq=tq),
        grid=(batch, dil, nq),
        in_specs=[
            pl.BlockSpec((tq, GROUP_W), cur),
            pl.BlockSpec((tq, GROUP_W), cur),
            pl.BlockSpec((BAND, GROUP_W), prv),
            pl.BlockSpec((tq, GROUP_W), cur),
            pl.BlockSpec((BAND, GROUP_W), prv),
        ],
        out_specs=[pl.BlockSpec((tq, GROUP_W), cur), pl.BlockSpec((tq, LANES), cur)],
        out_shape=[jax.ShapeDtypeStruct((rows, dil * GROUP_W), BF16),
                   jax.ShapeDtypeStruct((rows, dil * LANES), F32)],
        compiler_params=pltpu.CompilerParams(dimension_semantics=("arbitrary",) * 3, vmem_limit_bytes=VMEM_LIMIT),
        name=f"swa_d{dil}",
    )(qv, kv, kv, vv, vv)


T_PAD = 8


def _swa_sample_kernel(*refs, t_len, n_buf, n_alias):
    q_ref, kn_ref, vn_ref, ktail_ref, vtail_ref, kc_ref, vc_ref = refs[:7]
    nk_ref, nv_ref, o_ref, l_ref = refs[7 + n_alias:]
    t_row = lax.broadcasted_iota(I32, (T_PAD, n_buf), 0)
    delta_c = n_buf + t_row - lax.broadcasted_iota(I32, (T_PAD, n_buf), 1)
    t_row_n = lax.broadcasted_iota(I32, (T_PAD, LANES), 0)
    col_n = lax.broadcasted_iota(I32, (T_PAD, LANES), 1)
    delta_n = t_row_n - col_n
    lane = lax.broadcasted_iota(I32, (HEAD_DIM, LANES), 1)
    masks = []
    for window, dil in DILATIONS:
        masks.append(((delta_c >= 0) & ((delta_c & (dil - 1)) == 0) & (delta_c <= window),
                      (col_n < t_len) & (delta_n >= 0) & ((delta_n & (dil - 1)) == 0) & (delta_n <= window)))

    def x3(a_parts, b_parts, nt):
        (ah, al), (bh, bl) = a_parts, b_parts
        f = _dot_nt if nt else _dot
        r = f(jnp.concatenate([ah, al], axis=0), bh)
        return r[:T_PAD] + r[T_PAD:] + f(ah, bl)

    for h in range(N_HEADS):
        kt = kc_ref[h]
        vt = vc_ref[h]
        for src, tail_ref, dst in ((kt, ktail_ref, nk_ref), (vt, vtail_ref, nv_ref)):
            rolled = pltpu.roll(src, n_buf - t_len, 1)
            dst[h, :, 0:n_buf - LANES] = rolled[:, 0:n_buf - LANES]
            dst[h, :, n_buf - LANES:n_buf] = jnp.where(lane >= LANES - t_len, tail_ref[h], rolled[:, n_buf - LANES:])
        q_p = _split2(q_ref[h] * (HEAD_DIM ** -0.5))
        s_c = x3(q_p, _split2(kt), False)
        s_n = x3(q_p, _split2(kn_ref[h]), True)
        vt_p = _split2(vt)
        vn_p = _split2(vn_ref[h])
        for bi, (valid_c, valid_n) in enumerate(masks):
            c0 = max(0, (n_buf - DILATIONS[bi][0] - t_len) // LANES * LANES)
            sc = jnp.where(valid_c[:, c0:], s_c[:, c0:], NEG_BIG)
            sn = jnp.where(valid_n, s_n, NEG_BIG)
            m = jnp.maximum(jnp.max(sc, axis=1, keepdims=True), jnp.max(sn, axis=1, keepdims=True))
            pc = jnp.exp(sc - m)
            pn = jnp.exp(sn - m)
            den = jnp.sum(pc, axis=1, keepdims=True) + jnp.sum(pn, axis=1, keepdims=True)
            num = x3(_split2(pc), (vt_p[0][:, c0:], vt_p[1][:, c0:]), True) + x3(_split2(pn), vn_p, False)
            o_ref[h, bi * T_PAD:(bi + 1) * T_PAD, :] = num / den
            l_ref[h, bi * T_PAD:(bi + 1) * T_PAD, :] = jnp.broadcast_to(m + jnp.log(den), (T_PAD, LANES))


def _swa_sample_call(q, k_new, v_new, cache_kt, cache_vt, layer, new_kt_all, new_vt_all):
    nbatch, t_len = q.shape[:2]
    n_buf = cache_kt.shape[-1]
    nd = len(DILATIONS)
    assert t_len <= T_PAD and n_buf % LANES == 0 and n_buf >= max(w for w, _ in DILATIONS)

    def head_major(x, rows):
        return jnp.pad(x.transpose(0, 2, 1, 3), ((0, 0), (0, 0), (0, rows - t_len), (0, 0)))

    def tail(x):
        return jnp.pad(x.transpose(0, 2, 3, 1), ((0, 0), (0, 0), (0, 0), (LANES - t_len, 0)))

    def spec(rows, width):
        return pl.BlockSpec((None, N_HEADS, rows, width), lambda b: (b, 0, 0, 0))

    cache = pl.BlockSpec((None, None, N_HEADS, HEAD_DIM, n_buf), lambda b: (layer, b, 0, 0, 0))
    in_specs = [spec(T_PAD, HEAD_DIM), spec(LANES, HEAD_DIM), spec(LANES, HEAD_DIM), spec(HEAD_DIM, LANES),
                spec(HEAD_DIM, LANES), cache, cache]
    args = [head_major(q, T_PAD), head_major(k_new, LANES), head_major(v_new, LANES), tail(k_new), tail(v_new),
            cache_kt, cache_vt]
    aliases = {}
    if new_kt_all is not None:
        in_specs += [pl.BlockSpec(memory_space=pl.ANY)] * 2
        args += [new_kt_all, new_vt_all]
        aliases = {7: 0, 8: 1}
    return pl.pallas_call(
        functools.partial(_swa_sample_kernel, t_len=t_len, n_buf=n_buf, n_alias=len(aliases)),
        grid=(nbatch,),
        in_specs=in_specs,
        out_specs=[cache, cache, spec(nd * T_PAD, HEAD_DIM), spec(nd * T_PAD, LANES)],
        out_shape=[jax.ShapeDtypeStruct(cache_kt.shape, F32), jax.ShapeDtypeStruct(cache_vt.shape, F32),
                   jax.ShapeDtypeStruct((nbatch, N_HEADS, nd * T_PAD, HEAD_DIM), F32),
                   jax.ShapeDtypeStruct((nbatch, N_HEADS, nd * T_PAD, LANES), F32)],
        input_output_aliases=aliases,
        compiler_params=pltpu.CompilerParams(dimension_semantics=("arbitrary",), vmem_limit_bytes=VMEM_LIMIT),
        name="swa_sample",
    )(*args)


def _outproj_kernel(*refs, route, n_dtype, precise, dils):
    oa_ref = refs[0]
    o_refs = refs[1:4]
    l_refs = refs[4:7]
    x_ref, w_ref, g_ref = refs[7:10]
    nxt = 10
    if route:
        r_ref = refs[nxt]
        nxt += 1
    n_perm = sum(1 for d in dils if d > 1)
    perm_refs = refs[nxt:nxt + n_perm]
    nxt += n_perm
    if route:
        x1_ref, n_ref, ri_ref, rw_ref = refs[nxt:]
    else:
        x1_ref, n_ref = refs[nxt:]

    def token_order(ref, d, pi, width, exact_f32):
        if d == 1:
            return ref[...]
        stacked = jnp.concatenate([ref[:, r * width:(r + 1) * width] for r in range(d)], axis=0)
        pt = perm_refs[pi][...]
        if exact_f32:
            return sum(_dot(pt, part) for part in _split3(stacked))
        return _dot(pt, stacked)

    o_vals, lses, pi = [], [], 0
    for d, o_ref, l_ref in zip(dils, o_refs, l_refs):
        o_vals.append(token_order(o_ref, d, pi, GROUP_W, False))
        lses.append(token_order(l_ref, d, pi, LANES, True))
        pi += 1 if d > 1 else 0
    mx = jnp.maximum(jnp.maximum(lses[0], lses[1]), lses[2])
    es = [jnp.exp(l - mx) for l in lses]
    inv = 1.0 / (es[0] + es[1] + es[2])
    er = lax.broadcasted_iota(I32, (LANES, GROUP_W), 0)
    ec = lax.broadcasted_iota(I32, (LANES, GROUP_W), 1)
    expand = jnp.where(er == ec // HEAD_DIM, 1.0, 0.0).astype(BF16)
    ob = None
    for e, o_val in zip(es, o_vals):
        hi, lo = _split2(e * inv)
        term = (_dot(hi, expand) + _dot(lo, expand)) * o_val
        ob = term if ob is None else ob + term
    mixed = _mm(oa_ref[...], w_ref[0:GROUP_W, :], precise) + _mm(ob, w_ref[GROUP_W:, :], precise)
    x1 = x_ref[...] + mixed
    x1_ref[...] = x1
    n = _rmsnorm(x1, g_ref[...])
    n_ref[...] = n.astype(n_dtype)
    if route:
        logits = _mm(n, r_ref[...], True) if precise else _dot_x3(n, r_ref[...])
        lane = lax.broadcasted_iota(I32, logits.shape, 1)
        logits = jnp.where(lane < N_EXPERTS, logits, NEG_BIG)
        m1 = jnp.max(logits, axis=1, keepdims=True)
        i1 = jnp.min(jnp.where(logits == m1, lane, LANES), axis=1, keepdims=True)
        rest = jnp.where(lane == i1, NEG_BIG, logits)
        m2 = jnp.max(rest, axis=1, keepdims=True)
        i2 = jnp.min(jnp.where(rest == m2, lane, LANES), axis=1, keepdims=True)
        e2 = jnp.exp(m2 - m1)
        w1 = 1.0 / (1.0 + e2)
        w2 = e2 * w1
        ri_ref[...] = jnp.where(lane == 0, i1, jnp.where(lane == 1, i2, 0))
        rw_ref[...] = jnp.where(lane == 0, w1, jnp.where(lane == 1, w2, 0.0))


def _outproj_call(oa, o_br, l_br, x, w_out, g, router_pad, tm, n_dtype, precise=False, dils=(1, 1, 1)):
    m = x.shape[0]
    route = router_pad is not None
    row = lambda i: (i, 0)
    const = lambda i: (0, 0)
    in_specs = ([pl.BlockSpec((tm, GROUP_W), row)]
                + [pl.BlockSpec((tm // d, d * GROUP_W), row) for d in dils]
                + [pl.BlockSpec((tm // d, d * LANES), row) for d in dils]
                + [pl.BlockSpec((tm, D_MODEL), row), pl.BlockSpec((D_MODEL, D_MODEL), const),
                   pl.BlockSpec((1, D_MODEL), const)])
    args = [oa, *o_br, *l_br, x, w_out, g]
    if route:
        in_specs.append(pl.BlockSpec((D_MODEL, LANES), const))
        args.append(router_pad)
    for d in dils:
        if d > 1:
            in_specs.append(pl.BlockSpec((tm, tm), const))
            args.append(jnp.asarray(_residue_perm(tm, d).T, BF16))
    out_specs = [pl.BlockSpec((tm, D_MODEL), row)] * 2
    out_shape = [jax.ShapeDtypeStruct((m, D_MODEL), F32), jax.ShapeDtypeStruct((m, D_MODEL), n_dtype)]
    if route:
        out_specs += [pl.BlockSpec((tm, LANES), row)] * 2
        out_shape += [jax.ShapeDtypeStruct((m, LANES), I32), jax.ShapeDtypeStruct((m, LANES), F32)]
    return pl.pallas_call(
        functools.partial(_outproj_kernel, route=route, n_dtype=n_dtype, precise=precise, dils=tuple(dils)),
        grid=(m // tm,),
        in_specs=in_specs,
        out_specs=out_specs,
        out_shape=out_shape,
        compiler_params=pltpu.CompilerParams(dimension_semantics=("arbitrary",), vmem_limit_bytes=VMEM_LIMIT),
        name="outproj",
    )(*args)


def _ffn_kernel(n_ref, x1_ref, wg_ref, wu_ref, wd_ref, o_ref, *, f_chunk):
    n = n_ref[...]
    d_ff = wg_ref.shape[1]
    acc = x1_ref[...]
    for f0 in range(0, d_ff, f_chunk):
        h = _silu(_mm(n, wg_ref[:, f0:f0 + f_chunk])) * _mm(n, wu_ref[:, f0:f0 + f_chunk])
        acc = acc + _mm(h, wd_ref[f0:f0 + f_chunk, :])
    o_ref[...] = acc


def _ffn_call(n, x1, wg, wu, wd, tm):
    m = n.shape[0]
    d_ff = wg.shape[1]
    f_chunk = d_ff // 2 if (d_ff // 2) % LANES == 0 else d_ff
    row = lambda i: (i, 0)
    const = lambda i: (0, 0)
    return pl.pallas_call(
        functools.partial(_ffn_kernel, f_chunk=f_chunk),
        grid=(m // tm,),
        in_specs=[pl.BlockSpec((tm, D_MODEL), row), pl.BlockSpec((tm, D_MODEL), row),
                  pl.BlockSpec((D_MODEL, d_ff), const), pl.BlockSpec((D_MODEL, d_ff), const),
                  pl.BlockSpec((d_ff, D_MODEL), const)],
        out_specs=pl.BlockSpec((tm, D_MODEL), row),
        out_shape=jax.ShapeDtypeStruct((m, D_MODEL), F32),
        compiler_params=pltpu.CompilerParams(dimension_semantics=("arbitrary",), vmem_limit_bytes=VMEM_LIMIT),
        name="ffn_dense",
    )(n, x1, wg, wu, wd)


def _ffn_stream_kernel(n_ref, x1_ref, wg_ref, wu_ref, wd_ref, o_ref, *, precise):
    @pl.when(pl.program_id(0) == 0)
    def _():
        o_ref[...] = x1_ref[...]

    n = n_ref[...]
    h = _silu(_mm(n, wg_ref[...], precise)) * _mm(n, wu_ref[...], precise)
    o_ref[...] += _mm(h, wd_ref[...], precise)


def _ffn_stream_call(n, x1, wg, wu, wd, tf, precise):
    m = n.shape[0]
    d_ff = wg.shape[1]
    const = lambda j: (0, 0)
    return pl.pallas_call(
        functools.partial(_ffn_stream_kernel, precise=precise),
        grid=(d_ff // tf,),
        in_specs=[pl.BlockSpec((m, D_MODEL), const), pl.BlockSpec((m, D_MODEL), const),
                  pl.BlockSpec((D_MODEL, tf), lambda j: (0, j)), pl.BlockSpec((D_MODEL, tf), lambda j: (0, j)),
                  pl.BlockSpec((tf, D_MODEL), lambda j: (j, 0))],
        out_specs=pl.BlockSpec((m, D_MODEL), const),
        out_shape=jax.ShapeDtypeStruct((m, D_MODEL), F32),
        compiler_params=pltpu.CompilerParams(dimension_semantics=("arbitrary",), vmem_limit_bytes=VMEM_LIMIT),
        name="ffn_stream",
    )(n, x1, wg, wu, wd)


def _moe_kernel(te_ref, tv_ref, src_ref, nxt_ref, n_hbm, wg_ref, wu_ref, wd_ref, y_ref, x_s, xb_s, acc_s, sems, *,
                tme, nj):
    i = pl.program_id(0)
    j = pl.program_id(1)
    n_tiles = pl.num_programs(0)
    slot = i % 2
    share = tme // nj

    def tile_wait(s):
        pltpu.make_async_copy(n_hbm.at[pl.ds(0, tme), :], x_s.at[s], sems.at[s]).wait()

    @pl.when(tv_ref[i] > 0)
    def _():
        @pl.when((i == 0) & (j == 0))
        def _():
            def issue(r, c):
                pltpu.make_async_copy(n_hbm.at[pl.ds(src_ref[0, 0, r], 1), :], x_s.at[0, pl.ds(r, 1), :],
                                      sems.at[0]).start()
                return c
            lax.fori_loop(0, tme, issue, 0, unroll=8)

        @pl.when(j == 0)
        def _():
            tile_wait(slot)
            xb_s[...] = x_s[slot].astype(BF16)
            acc_s[...] = jnp.zeros_like(acc_s)

        x = xb_s[...]
        h = _silu(_mm(x, wg_ref[0])) * _mm(x, wu_ref[0])
        acc_s[...] += _mm(h, wd_ref[0])
        for k in range(share):
            r = j * share + k
            pltpu.make_async_copy(n_hbm.at[pl.ds(nxt_ref[0, 0, r], 1), :], x_s.at[1 - slot, pl.ds(r, 1), :],
                                  sems.at[1 - slot]).start()

        @pl.when(j == nj - 1)
        def _():
            y_ref[...] = acc_s[...]

            @pl.when(tv_ref[jnp.minimum(i + 1, n_tiles - 1)] * (i + 1 < n_tiles).astype(I32) == 0)
            def _():
                tile_wait(1 - slot)

    @pl.when((tv_ref[i] == 0) & (j == nj - 1))
    def _():
        y_ref[...] = jnp.zeros_like(y_ref)


def _moe_call(n_all, tile_expert, tile_valid, src_tok3, wg, wu, wd, tme, tf):
    n_tiles = tile_expert.shape[0]
    d_ff = wg.shape[2]
    nj = d_ff // tf
    assert tme % nj == 0 and tme % 8 == 0

    def jsel(i, j, te, tv):
        return jnp.where(tv[i] > 0, j, nj - 1)

    grid_spec = pltpu.PrefetchScalarGridSpec(
        num_scalar_prefetch=2,
        grid=(n_tiles, nj),
        in_specs=[
            pl.BlockSpec((1, 1, tme), lambda i, j, te, tv: (i, 0, 0), memory_space=pltpu.SMEM),
            pl.BlockSpec((1, 1, tme), lambda i, j, te, tv: (jnp.minimum(i + 1, n_tiles - 1), 0, 0),
                         memory_space=pltpu.SMEM),
            pl.BlockSpec(memory_space=pl.ANY),
            pl.BlockSpec((1, D_MODEL, tf), lambda i, j, te, tv: (te[i], 0, jsel(i, j, te, tv))),
            pl.BlockSpec((1, D_MODEL, tf), lambda i, j, te, tv: (te[i], 0, jsel(i, j, te, tv))),
            pl.BlockSpec((1, tf, D_MODEL), lambda i, j, te, tv: (te[i], jsel(i, j, te, tv), 0)),
        ],
        out_specs=pl.BlockSpec((tme, D_MODEL), lambda i, j, te, tv: (i, 0)),
        scratch_shapes=[pltpu.VMEM((2, tme, D_MODEL), F32), pltpu.VMEM((tme, D_MODEL), BF16),
                        pltpu.VMEM((tme, D_MODEL), F32), pltpu.SemaphoreType.DMA((2,))],
    )
    return pl.pallas_call(
        functools.partial(_moe_kernel, tme=tme, nj=nj),
        grid_spec=grid_spec,
        out_shape=jax.ShapeDtypeStruct((n_tiles * tme, D_MODEL), F32),
        compiler_params=pltpu.CompilerParams(dimension_semantics=("arbitrary", "arbitrary"), vmem_limit_bytes=VMEM_LIMIT),
        name="moe_experts",
    )(tile_expert, tile_valid, src_tok3, src_tok3, n_all, wg, wu, wd)


def _route_plan(route_idx, tme, n_tiles):
    n_tok = route_idx.shape[0]
    e_flat = route_idx.reshape(-1)
    onehot = (e_flat[:, None] == jnp.arange(N_EXPERTS, dtype=I32)[None, :]).astype(I32)
    csum = jnp.cumsum(onehot, axis=0)
    rank = jnp.sum(csum * onehot, axis=1) - 1
    cnt = csum[-1]
    tiles_e = (cnt + tme - 1) // tme
    tile_end = jnp.cumsum(tiles_e)
    row_off = (tile_end - tiles_e) * tme
    dst = jnp.sum(onehot * row_off[None, :], axis=1) + rank
    tile_id = jnp.arange(n_tiles, dtype=I32)
    tile_expert = jnp.minimum(jnp.sum((tile_id[:, None] >= tile_end[None, :]).astype(I32), axis=1), N_EXPERTS - 1)
    tile_valid = (tile_id < tile_end[-1]).astype(I32)
    tok_of = jnp.arange(2 * n_tok, dtype=I32) // 2
    src_tok = jnp.zeros((n_tiles * tme,), I32).at[dst].set(tok_of)
    return dst.reshape(n_tok, 2), src_tok.reshape(n_tiles, 1, tme), tile_expert.astype(I32), tile_valid


def _moe_combine_kernel(pos_ref, y_hbm, x1_ref, rw_ref, g_ref, o_ref, y_s, sem, *, tm, final_norm):
    def issue(half, c):
        for prio in range(2):
            r = 2 * half + prio
            for slot in range(2):
                pos = pos_ref[0, 0, 2 * r + slot]
                pltpu.make_async_copy(y_hbm.at[pl.ds(pos, 1), :], y_s.at[slot, pl.ds(r, 1), :], sem).start(
                    priority=prio)
        return c
    lax.fori_loop(0, tm // 2, issue, 0, unroll=4)
    for slot in range(2):
        pltpu.make_async_copy(y_hbm.at[pl.ds(0, tm), :], y_s.at[slot], sem).wait()
    rw = rw_ref[...]
    out = x1_ref[...] + rw[:, 0:1] * y_s[0] + rw[:, 1:2] * y_s[1]
    if final_norm:
        out = _rmsnorm(out, g_ref[...])
    o_ref[...] = out


def _moe_combine_call(pos3, y_sorted, x1, rw, g_final, tm, final_norm):
    m = x1.shape[0]
    row = lambda i: (i, 0)
    return pl.pallas_call(
        functools.partial(_moe_combine_kernel, tm=tm, final_norm=final_norm),
        grid=(m // tm,),
        in_specs=[
            pl.BlockSpec((1, 1, 2 * tm), lambda i: (i, 0, 0), memory_space=pltpu.SMEM),
            pl.BlockSpec(memory_space=pl.ANY),
            pl.BlockSpec((tm, D_MODEL), row),
            pl.BlockSpec((tm, LANES), row),
            pl.BlockSpec((1, D_MODEL), lambda i: (0, 0)),
        ],
        out_specs=pl.BlockSpec((tm, D_MODEL), row),
        out_shape=jax.ShapeDtypeStruct((m, D_MODEL), F32),
        scratch_shapes=[pltpu.VMEM((2, tm, D_MODEL), F32), pltpu.SemaphoreType.DMA],
        compiler_params=pltpu.CompilerParams(dimension_semantics=("arbitrary",), vmem_limit_bytes=VMEM_LIMIT),
        name="moe_combine",
    )(pos3, y_sorted, x1, rw, g_final)


def _final_norm_kernel(x_ref, g_ref, o_ref):
    o_ref[...] = _rmsnorm(x_ref[...], g_ref[...])


def _final_norm_call(x, g, tm):
    m = x.shape[0]
    row = lambda i: (i, 0)
    return pl.pallas_call(
        _final_norm_kernel,
        grid=(m // tm,),
        in_specs=[pl.BlockSpec((tm, D_MODEL), row), pl.BlockSpec((1, D_MODEL), lambda i: (0, 0))],
        out_specs=pl.BlockSpec((tm, D_MODEL), row),
        out_shape=jax.ShapeDtypeStruct((m, D_MODEL), F32),
        compiler_params=pltpu.CompilerParams(dimension_semantics=("arbitrary",)),
        name="final_norm",
    )(x, g)


def _rope_tables(pos):
    half = HEAD_DIM // 2
    inv_freq = np.power(np.float64(ROPE_THETA), -np.arange(half, dtype=np.float64) * (2.0 / HEAD_DIM))
    ang = np.asarray(pos, np.float64)[:, None] * inv_freq[None, :]
    c, s = np.cos(ang), np.sin(ang)
    cos_t = np.concatenate([c, c, c, c], axis=1).astype(np.float32)
    sin_t = np.concatenate([-s, s, -s, s], axis=1).astype(np.float32)
    return jnp.asarray(cos_t), jnp.asarray(sin_t)


def _pack_w_in(w):
    gates0 = QKV_W + GROUP_W
    pad = jnp.zeros((w.shape[0], LANES - 2 * N_HEADS), w.dtype)
    return jnp.concatenate([w[:, :gates0], w[:, gates0 + 2 * N_HEADS:], w[:, gates0:gates0 + 2 * N_HEADS], pad],
                           axis=1)


def _lane_row(vals, offset):
    return jnp.zeros((1, LANES), F32).at[0, offset:offset + vals.shape[0]].set(vals.astype(F32))


def _pad_rows(x, rows, front=0):
    return jnp.pad(x, ((0, 0), (front, rows - front - x.shape[1]), (0, 0)))


def kernel(x_prompt, x_sample, state_conv, state_delta, cache_win_k, cache_win_v, norm_mix, w_in, conv_w, a_log,
           dt_bias, gdn_norm, w_out, norm_ffn, ffn_gate, ffn_up, ffn_down, router, moe_gate, moe_up, moe_down,
           final_norm):
    bp, seq, _ = x_prompt.shape
    bs, t_len, _ = x_sample.shape
    depth = w_in.shape[0]
    n_p, n_s = bp * seq, bs * t_len
    n_all = n_p + n_s
    tm_p, tm_s = 512, n_s
    gdn_rows = 256
    n_buf = cache_win_k.shape[2]
    n_keep = min(WINDOW_MAX, seq)

    cos_p, sin_p = _rope_tables(np.arange(seq))
    cos_s, sin_s = _rope_tables(PAST_LEN + (np.arange(n_s) % t_len))
    xp = x_prompt.reshape(n_p, D_MODEL)
    xs = x_sample.reshape(n_s, D_MODEL)
    g_final = final_norm.reshape(1, D_MODEL)

    conv_p, delta_p, wk_p, wv_p = [], [], [], []
    conv_s, delta_s = [], []
    new_k_all = new_v_all = None
    cache_kt = cache_win_k.astype(F32).transpose(0, 1, 3, 4, 2)
    cache_vt = cache_win_v.astype(F32).transpose(0, 1, 3, 4, 2)
    for l in range(depth):
        w_packed_f = _pack_w_in(w_in[l].astype(F32))
        w_packed = w_packed_f.astype(BF16)
        w_out_f = w_out[l].astype(F32)
        g_mix = norm_mix[l].reshape(1, D_MODEL)
        alog_row = _lane_row(a_log[l], N_HEADS)
        dtb_row = _lane_row(dt_bias[l], N_HEADS)
        gn_row = jnp.tile(gdn_norm[l].astype(F32), N_HEADS).reshape(1, GROUP_W)
        w_out_b = w_out[l].astype(BF16)
        g_ffn = norm_ffn[l].reshape(1, D_MODEL)
        is_moe = l % 2 == 1
        idx = l // 2

        dils = tuple(d for _, d in DILATIONS)
        qkv, z, gt, k, v, *qkv_d = _inproj_call(xp, g_mix, w_packed, cos_p, sin_p, tm_p, dils=dils)
        oa, s_bd = _gdn_prompt_call(qkv, z, gt, conv_w[l], alog_row, dtb_row, gn_row, bp, seq, gdn_rows)
        branches = [_swa_call(*qkv_d[3 * i:3 * i + 3], dil, bp, seq) for i, dil in enumerate(dils)]
        conv_p.append(qkv.reshape(bp, seq, QKV_W)[:, seq - (CONV_TAPS - 1):])
        delta_p.append(_state_from_block_diag(s_bd))
        wk_p.append(k.reshape(bp, seq, GROUP_W)[:, seq - n_keep:].reshape(bp, n_keep, N_HEADS, HEAD_DIM))
        wv_p.append(v.reshape(bp, seq, GROUP_W)[:, seq - n_keep:].reshape(bp, n_keep, N_HEADS, HEAD_DIM))

        qkv_s, z_s, gt_s, q_s, k_s, v_s = _inproj_call(xs, g_mix, w_packed_f, cos_s, sin_s, tm_s, precise=True)
        prev_pad = _pad_rows(state_conv[l].astype(F32), 8, front=8 - (CONV_TAPS - 1))
        oa_s_pad, s_bd_s = _gdn_sample_call(
            _pad_rows(qkv_s.reshape(bs, t_len, QKV_W), CHUNK), prev_pad,
            _pad_rows(z_s.reshape(bs, t_len, GROUP_W), CHUNK), _pad_rows(gt_s.reshape(bs, t_len, LANES), CHUNK),
            conv_w[l], alog_row, dtb_row, gn_row, _state_to_block_diag(state_delta[l]), t_len, min(8, bs))
        oa_s = oa_s_pad[:, :t_len].reshape(n_s, GROUP_W)
        heads = (bs, t_len, N_HEADS, HEAD_DIM)
        new_k_all, new_v_all, o_s3, l_s3 = _swa_sample_call(
            q_s.reshape(heads), k_s.reshape(heads), v_s.reshape(heads), cache_kt, cache_vt, l, new_k_all, new_v_all)
        conv_all = jnp.concatenate([state_conv[l].astype(F32), qkv_s.reshape(bs, t_len, QKV_W)], axis=1)
        conv_s.append(conv_all[:, conv_all.shape[1] - (CONV_TAPS - 1):])
        delta_s.append(_state_from_block_diag(s_bd_s))
        nd = len(DILATIONS)
        o_br_s = [o_s3[:, :, i * T_PAD:i * T_PAD + t_len].transpose(0, 2, 1, 3).reshape(n_s, GROUP_W)
                  for i in range(nd)]
        l_br_s = [jnp.pad(l_s3[:, :, i * T_PAD:i * T_PAD + t_len, 0].transpose(0, 2, 1).reshape(n_s, N_HEADS),
                          ((0, 0), (0, LANES - N_HEADS))) for i in range(nd)]

        o_br_p = [b[0] for b in branches]
        l_br_p = [b[1] for b in branches]
        if not is_moe:
            x1p, np_ = _outproj_call(oa.reshape(n_p, GROUP_W), o_br_p, l_br_p, xp, w_out_b, g_ffn, None, tm_p, BF16,
                                     dils=dils)
            x1s, ns_ = _outproj_call(oa_s, o_br_s, l_br_s, xs, w_out_f, g_ffn, None, tm_s, F32, precise=True)
            wg, wu, wd = ffn_gate[idx].astype(BF16), ffn_up[idx].astype(BF16), ffn_down[idx].astype(BF16)
            xp = _ffn_call(np_, x1p, wg, wu, wd, tm_p)
            xs = _ffn_stream_call(ns_, x1s, ffn_gate[idx].astype(F32), ffn_up[idx].astype(F32),
                                  ffn_down[idx].astype(F32), 256, precise=True)
            if l == depth - 1:
                xp = _final_norm_call(xp, g_final, tm_p)
                xs = _final_norm_call(xs, g_final, tm_s)
        else:
            router_pad = jnp.pad(router[idx].astype(F32), ((0, 0), (0, LANES - N_EXPERTS)))
            x1p, n_tok_p, ri_p, rw_p = _outproj_call(
                oa.reshape(n_p, GROUP_W), o_br_p, l_br_p, xp, w_out_b, g_ffn, router_pad, tm_p, F32, dils=dils)
            x1s, n_tok_s, ri_s, rw_s = _outproj_call(oa_s, o_br_s, l_br_s, xs, w_out_f, g_ffn, router_pad, tm_s, F32,
                                                     precise=True)
            n_tok = jnp.concatenate([n_tok_p, n_tok_s], axis=0)
            tme, tf = 1120, 512
            n_tiles = (2 * n_all + N_EXPERTS * (tme - 1)) // tme
            ridx = jnp.concatenate([ri_p[:, :2], ri_s[:, :2]], axis=0)
            dst, src_tok3, tile_expert, tile_valid = _route_plan(ridx, tme, n_tiles)
            y_sorted = _moe_call(n_tok, tile_expert, tile_valid, src_tok3, moe_gate[idx], moe_up[idx], moe_down[idx],
                                 tme, tf)
            last = l == depth - 1
            tmc_p, tmc_s = 256, n_s
            xp = _moe_combine_call(dst[:n_p].reshape(n_p // tmc_p, 1, 2 * tmc_p), y_sorted, x1p, rw_p, g_final,
                                   tmc_p, last)
            xs = _moe_combine_call(dst[n_p:].reshape(n_s // tmc_s, 1, 2 * tmc_s), y_sorted, x1s, rw_s, g_final,
                                   tmc_s, last)

    y_prompt = xp.reshape(bp, seq, D_MODEL)
    y_sample = xs.reshape(bs, t_len, D_MODEL)
    return (y_prompt, y_sample, jnp.stack(conv_p), jnp.stack(delta_p), jnp.stack(wk_p), jnp.stack(wv_p),
            jnp.stack(conv_s), jnp.stack(delta_s), new_k_all.transpose(0, 1, 4, 2, 3),
            new_v_all.transpose(0, 1, 4, 2, 3))
```

```python
import functools

import numpy as np
import jax
import jax.numpy as jnp
from jax import lax
from jax.experimental import pallas as pl
from jax.experimental.pallas import tpu as pltpu

F32 = jnp.float32
BF16 = jnp.bfloat16
I32 = jnp.int32

EPS = 1e-6
NEG_BIG = -1e30
D_MODEL = 1024
HEAD_DIM = 64
N_HEADS = 8
GROUP_W = N_HEADS * HEAD_DIM
QKV_W = 3 * GROUP_W
CONV_TAPS = 4
CHUNK = 64
HEADS_PER_PACK = 4
PACK_W = HEADS_PER_PACK * HEAD_DIM
DILATIONS = ((128, 1), (512, 4), (2048, 16))
BAND = 128
WINDOW_MAX = 2048
ROPE_THETA = 10000.0
PAST_LEN = 16384
N_EXPERTS = 8
PROJ_PACKED_W = QKV_W + 4 * GROUP_W + 128
LANES = 128
VMEM_LIMIT = 56 * 1024 * 1024


def _dot(a, b):
    return jnp.dot(a, b, preferred_element_type=F32)


def _dot_nt(a, b):
    return lax.dot_general(a, b, (((1,), (1,)), ((), ())), preferred_element_type=F32)


def _mm(a, b, precise=False):
    if precise:
        return jnp.dot(a.astype(F32), b.astype(F32), preferred_element_type=F32, precision=lax.Precision.HIGHEST)
    return jnp.dot(a.astype(BF16), b.astype(BF16), preferred_element_type=F32)


def _split2(x):
    hi = x.astype(BF16)
    lo = (x - hi.astype(F32)).astype(BF16)
    return hi, lo


def _split3(x):
    hi = x.astype(BF16)
    r = x - hi.astype(F32)
    mid = r.astype(BF16)
    lo = (r - mid.astype(F32)).astype(BF16)
    return hi, mid, lo


def _dot_x3(a, b):
    ah, al = _split2(a)
    bh, bl = _split2(b)
    m = a.shape[0]
    r = _dot(jnp.concatenate([ah, al], axis=0), bh)
    return r[:m] + r[m:] + _dot(ah, bl)


def _dot_exact_rhs01(x, e):
    return sum(_dot(p, e) for p in _split3(x))


def _sigmoid(x):
    return 1.0 / (1.0 + jnp.exp(-x))


def _silu(x):
    return x * _sigmoid(x)


def _softplus(x):
    return jnp.maximum(x, 0.0) + jnp.log1p(jnp.exp(-jnp.abs(x)))


def _rmsnorm(x, g):
    ms = jnp.mean(x * x, axis=-1, keepdims=True)
    return x * lax.rsqrt(ms + EPS) * g


def _head_ones(width):
    r = lax.broadcasted_iota(I32, (width, width), 0) // HEAD_DIM
    c = lax.broadcasted_iota(I32, (width, width), 1) // HEAD_DIM
    return jnp.where(r == c, 1.0, 0.0).astype(BF16)


def _head_sum(x, ones_bd, precise=True):
    if not precise:
        return _dot(x.astype(BF16), ones_bd)
    hi, lo = _split2(x)
    return _dot(hi, ones_bd) + _dot(lo, ones_bd)


def _rope(x, cos, sin):
    n = x.shape[1]
    reps = n // LANES
    c = jnp.concatenate([cos] * reps, axis=1)
    s = jnp.concatenate([sin] * reps, axis=1)
    lane = lax.broadcasted_iota(I32, x.shape, 1)
    first_half = (lane & (HEAD_DIM - 1)) < HEAD_DIM // 2
    partner = jnp.where(first_half, pltpu.roll(x, n - HEAD_DIM // 2, 1), pltpu.roll(x, HEAD_DIM // 2, 1))
    return x * c + partner * s


def _residue_perm(tm, dil):
    i = np.arange(tm)
    src = (i % (tm // dil)) * dil + i // (tm // dil)
    p = np.zeros((tm, tm), np.float32)
    p[i, src] = 1.0
    return p


def _inproj_kernel(x_ref, g_ref, w_ref, cos_ref, sin_ref, *rest, precise, dils):
    y = _rmsnorm(x_ref[...], g_ref[...])
    if not precise:
        y = y.astype(BF16)

    def mm(lo, hi):
        return _mm(y, w_ref[:, lo:hi], precise)

    cos = cos_ref[...]
    sin = sin_ref[...]
    n_perm = 0 if dils is None else sum(1 for d in dils if d > 1)
    perm_refs, outs = rest[:n_perm], rest[n_perm:]
    o = 0
    outs[0][...] = mm(o, o + QKV_W)
    o += QKV_W
    outs[1][...] = mm(o, o + GROUP_W)
    o += GROUP_W
    q = _rope(mm(o, o + GROUP_W), cos, sin)
    o += GROUP_W
    k = _rope(mm(o, o + GROUP_W), cos, sin)
    o += GROUP_W
    v = mm(o, o + GROUP_W)
    o += GROUP_W
    outs[2][...] = mm(o, o + LANES)
    if dils is None:
        outs[3][...] = q
        outs[4][...] = k
        outs[5][...] = v
        return
    outs[3][...] = k
    outs[4][...] = v
    tm = q.shape[0]
    qkv_b = [t.astype(BF16) for t in (q, k, v)]
    nxt, pi = 5, 0
    for d in dils:
        if d == 1:
            for t in qkv_b:
                outs[nxt][...] = t
                nxt += 1
            continue
        sub = tm // d
        for t in qkv_b:
            tp = _dot(perm_refs[pi][...], t).astype(BF16)
            for r in range(d):
                outs[nxt][:, r * GROUP_W:(r + 1) * GROUP_W] = tp[r * sub:(r + 1) * sub, :]
            nxt += 1
        pi += 1


def _inproj_call(x, g, w_packed, cos_tab, sin_tab, tm, precise=False, dils=None):
    m = x.shape[0]
    n_tab = cos_tab.shape[0] // tm
    row = lambda i: (i, 0)
    const = lambda i: (0, 0)
    tab = lambda i: (i % n_tab, 0)
    in_specs = [
        pl.BlockSpec((tm, D_MODEL), row),
        pl.BlockSpec((1, D_MODEL), const),
        pl.BlockSpec((D_MODEL, PROJ_PACKED_W), const),
        pl.BlockSpec((tm, LANES), tab),
        pl.BlockSpec((tm, LANES), tab),
    ]
    args = [x, g, w_packed, cos_tab, sin_tab]
    outs = [(m, QKV_W, tm, F32), (m, GROUP_W, tm, F32), (m, LANES, tm, F32)]
    if dils is None:
        outs += [(m, GROUP_W, tm, F32)] * 3
    else:
        outs += [(m, GROUP_W, tm, F32)] * 2
        for d in dils:
            outs += [(m // d, d * GROUP_W, tm // d, BF16)] * 3
            if d > 1:
                in_specs.append(pl.BlockSpec((tm, tm), const))
                args.append(jnp.asarray(_residue_perm(tm, d), BF16))
    return pl.pallas_call(
        functools.partial(_inproj_kernel, precise=precise, dils=dils),
        grid=(m // tm,),
        in_specs=in_specs,
        out_specs=[pl.BlockSpec((rows, w), row) for _, w, rows, _ in outs],
        out_shape=[jax.ShapeDtypeStruct((n, w), dt) for n, w, _, dt in outs],
        compiler_params=pltpu.CompilerParams(dimension_semantics=("arbitrary",), vmem_limit_bytes=VMEM_LIMIT),
        name="inproj",
    )(*args)


def _pack_masks():
    lane = lax.broadcasted_iota(I32, (1, PACK_W), 1)
    return [lane // HEAD_DIM == h for h in range(HEADS_PER_PACK)]


def _gdn_chunk_consts():
    i = lax.broadcasted_iota(I32, (CHUNK, PACK_W), 0)
    j = lax.broadcasted_iota(I32, (CHUNK, PACK_W), 1) & (HEAD_DIM - 1)
    r = lax.broadcasted_iota(I32, (CHUNK, CHUNK), 0)
    c = lax.broadcasted_iota(I32, (CHUNK, CHUNK), 1)
    br = lax.broadcasted_iota(I32, (PACK_W, PACK_W), 0) // HEAD_DIM
    bc = lax.broadcasted_iota(I32, (PACK_W, PACK_W), 1) // HEAD_DIM
    return dict(
        head_masks=_pack_masks(),
        lower=i >= j,
        strict=i > j,
        upper_f=jnp.where(i <= j, 1.0, 0.0),
        eye=jnp.where(i == j, 1.0, 0.0),
        tri=jnp.where(r >= c, 1.0, 0.0).astype(BF16),
        block=br == bc,
    )


def _bmm_dn(a, b, dn, precise):
    if not precise:
        return lax.dot_general(a.astype(BF16), b.astype(BF16), dn, preferred_element_type=F32)
    ah, al = _split2(a.astype(F32))
    bh, bl = _split2(b.astype(F32))
    m = a.shape[1]
    r = lax.dot_general(jnp.concatenate([ah, al], axis=1), bh, dn, preferred_element_type=F32)
    return r[:, :m] + r[:, m:] + lax.dot_general(ah, bl, dn, preferred_element_type=F32)


def _bmm(a, b, precise=False):
    return _bmm_dn(a, b, (((2,), (1,)), ((0,), (0,))), precise)


def _bmm_nt(a, b, precise=False):
    return _bmm_dn(a, b, (((2,), (2,)), ((0,), (0,))), precise)


def _block_diag_b(x, head_masks):
    return jnp.concatenate([jnp.where(m, x, 0.0) for m in head_masks], axis=1)


def _gdn_local(qn, kn, v, bw, gw, cst, precise):
    hm = cst["head_masks"]
    nb = qn.shape[0]
    tri = jnp.broadcast_to(cst["tri"], (nb, CHUNK, CHUNK))
    gcum = sum(_bmm(tri, p) for p in _split3(gw))
    grow = jnp.sum(gw * cst["upper_f"], axis=1, keepdims=True)
    diff = gcum - grow
    decay = jnp.where(cst["lower"], jnp.exp(jnp.where(cst["lower"], diff, 0.0)), 0.0)
    kq = _bmm_nt(jnp.concatenate([kn, qn], axis=1), _block_diag_b(kn, hm), precise)
    kk, qk = kq[:, :CHUNK], kq[:, CHUNK:]
    nmat = bw * kk * jnp.where(cst["strict"], decay, 0.0)
    p = -nmat
    y = cst["eye"] + p
    q = _bmm(p, _block_diag_b(p, hm), precise)
    n_sq = CHUNK.bit_length() - 1
    for s in range(1, n_sq):
        q_bd = _block_diag_b(q, hm)
        if s < n_sq - 1:
            both = _bmm(jnp.concatenate([y, q], axis=1), q_bd, precise)
            y = y + both[:, :CHUNK]
            q = both[:, CHUNK:]
        else:
            y = y + _bmm(y, q_bd, precise)
    egc = jnp.exp(gcum)
    rhs = jnp.concatenate([_block_diag_b(bw * v, hm), _block_diag_b(bw * egc * kn, hm)], axis=2)
    sol = _bmm(y, rhs, precise)
    glast = gcum[:, CHUNK - 1:CHUNK, :]
    return (sol[:, :, :PACK_W], sol[:, :, PACK_W:], qk * decay, qn * egc, kn * jnp.exp(glast - gcum),
            jnp.exp(glast))


def _gdn_scan_step(u, wk, aqk, qd, kd, gl, state, cst, precise):
    hm = cst["head_masks"]
    m1 = _bmm(jnp.concatenate([wk, qd], axis=1), state, precise)
    w = u - m1[:, :CHUNK]
    o = m1[:, CHUNK:] + _bmm(aqk, _block_diag_b(w, hm), precise)
    inc = _bmm(jnp.swapaxes(kd, 1, 2), w, precise)
    return o, state * gl + jnp.where(cst["block"], inc, 0.0)


def _gdn_kernel(*refs, nb, rows, carry, n_valid, precise):
    if carry:
        (qkv_ref, prev_ref, z_ref, gt_ref, cw_ref, al_ref, db_ref, gn_ref, o_ref, sout_ref, xp_s, st_s) = refs
        s0_ref = None
    else:
        (qkv_ref, prev_ref, z_ref, gt_ref, cw_ref, al_ref, db_ref, gn_ref, s0_ref, o_ref, sout_ref, xp_s, st_s) = refs
    step = pl.program_id(0)
    ones_bd = _head_ones(GROUP_W)
    lane = lax.broadcasted_iota(I32, (1, LANES), 1)
    er = lax.broadcasted_iota(I32, (LANES, 2 * GROUP_W), 0)
    ec = lax.broadcasted_iota(I32, (LANES, 2 * GROUP_W), 1)
    expand = jnp.where(((ec < GROUP_W) & (er == ec // HEAD_DIM))
                       | ((ec >= GROUP_W) & (er == N_HEADS + (ec - GROUP_W) // HEAD_DIM)), 1.0, 0.0).astype(BF16)
    cw = cw_ref[...]

    if carry:
        @pl.when(step == 0)
        def _():
            st_s[...] = jnp.zeros_like(st_s)
    else:
        st_s[...] = s0_ref[...].reshape(st_s.shape)

    per_seq = []
    for b in range(nb):
        prev = prev_ref[b]
        if carry:
            prev = jnp.where(step == 0, 0.0, prev)
        xp_s[b, 0:8, :] = prev
        xp_s[b, 8:8 + rows, :] = qkv_ref[b]
        acc = cw[CONV_TAPS - 1:CONV_TAPS, :] * xp_s[b, 8:8 + rows, :]
        for t in range(1, CONV_TAPS):
            acc = acc + cw[CONV_TAPS - 1 - t:CONV_TAPS - t, :] * xp_s[b, 8 - t:8 - t + rows, :]
        cq = _silu(acc)
        q = cq[:, :GROUP_W]
        k = cq[:, GROUP_W:2 * GROUP_W]
        v = cq[:, 2 * GROUP_W:]
        qn = q * lax.rsqrt(_head_sum(q * q, ones_bd, precise) + EPS) * (HEAD_DIM ** -0.5)
        kn = k * lax.rsqrt(_head_sum(k * k, ones_bd, precise) + EPS)
        t_raw = gt_ref[b]
        beta = _sigmoid(t_raw)
        gdec = -jnp.exp(al_ref[...]) * _softplus(t_raw + db_ref[...])
        wide = _dot_exact_rhs01(jnp.where(lane < N_HEADS, beta, gdec), expand)
        bw = wide[:, :GROUP_W]
        gw = wide[:, GROUP_W:]
        if n_valid < rows:
            live = lax.broadcasted_iota(I32, (rows, 1), 0) < n_valid
            bw = jnp.where(live, bw, 0.0)
            gw = jnp.where(live, gw, 0.0)
            qn = jnp.where(live, qn, 0.0)
            kn = jnp.where(live, kn, 0.0)
            v = jnp.where(live, v, 0.0)
        per_seq.append((qn, kn, v, bw, gw))

    cst = _gdn_chunk_consts()
    n_chunks = rows // CHUNK
    n_pack = N_HEADS // HEADS_PER_PACK

    def instances(idx):
        return jnp.stack([per_seq[b][idx][c * CHUNK:(c + 1) * CHUNK, g * PACK_W:(g + 1) * PACK_W]
                          for c in range(n_chunks) for b in range(nb) for g in range(n_pack)], axis=0)

    u, wk, aqk, qd, kd, gl = _gdn_local(*(instances(i) for i in range(5)), cst, precise)
    n_chain = nb * n_pack
    state = st_s[...]
    o_chunks = []
    for c in range(n_chunks):
        sl = slice(c * n_chain, (c + 1) * n_chain)
        o_c, state = _gdn_scan_step(u[sl], wk[sl], aqk[sl], qd[sl], kd[sl], gl[sl], state, cst, precise)
        o_chunks.append(o_c)
    st_s[...] = state

    for b in range(nb):
        o = jnp.concatenate(
            [jnp.concatenate([o_chunks[c][b * n_pack + g] for g in range(n_pack)], axis=1) for c in range(n_chunks)],
            axis=0)
        ms = _head_sum(o * o, ones_bd, precise) * (1.0 / HEAD_DIM)
        o_ref[b] = (o * lax.rsqrt(ms + EPS) * gn_ref[...] * _silu(z_ref[b])).astype(o_ref.dtype)

    if carry:
        @pl.when(step == pl.num_programs(0) - 1)
        def _():
            sout_ref[...] = state.reshape(sout_ref.shape)
    else:
        sout_ref[...] = state.reshape(sout_ref.shape)


def _gdn_scratch(nb, rows):
    return [pltpu.VMEM((nb, rows + 8, QKV_W), F32),
            pltpu.VMEM((nb * (N_HEADS // HEADS_PER_PACK), PACK_W, PACK_W), F32)]


def _gdn_prompt_call(qkv, z, gt, conv_w, alog_row, dtb_row, gn_row, batch, seq, rows):
    qkv3 = qkv.reshape(batch, seq, QKV_W)
    z3 = z.reshape(batch, seq, GROUP_W)
    gt3 = gt.reshape(batch, seq, LANES)
    blk = lambda i: (0, i, 0)
    const2 = lambda i: (0, 0)
    n_pack = N_HEADS // HEADS_PER_PACK
    return pl.pallas_call(
        functools.partial(_gdn_kernel, nb=batch, rows=rows, carry=True, n_valid=rows, precise=False),
        grid=(seq // rows,),
        in_specs=[
            pl.BlockSpec((batch, rows, QKV_W), blk),
            pl.BlockSpec((batch, 8, QKV_W), lambda i: (0, jnp.maximum(i * (rows // 8) - 1, 0), 0)),
            pl.BlockSpec((batch, rows, GROUP_W), blk),
            pl.BlockSpec((batch, rows, LANES), blk),
            pl.BlockSpec((CONV_TAPS, QKV_W), const2),
            pl.BlockSpec((1, LANES), const2),
            pl.BlockSpec((1, LANES), const2),
            pl.BlockSpec((1, GROUP_W), const2),
        ],
        out_specs=[
            pl.BlockSpec((batch, rows, GROUP_W), blk),
            pl.BlockSpec((batch, n_pack, PACK_W, PACK_W), lambda i: (0, 0, 0, 0)),
        ],
        out_shape=[
            jax.ShapeDtypeStruct((batch, seq, GROUP_W), BF16),
            jax.ShapeDtypeStruct((batch, n_pack, PACK_W, PACK_W), F32),
        ],
        scratch_shapes=_gdn_scratch(batch, rows),
        compiler_params=pltpu.CompilerParams(dimension_semantics=("arbitrary",), vmem_limit_bytes=VMEM_LIMIT),
        name="gdn_prompt",
    )(qkv3, qkv3, z3, gt3, conv_w, alog_row, dtb_row, gn_row)


def _gdn_sample_call(qkv_pad, prev_pad, z_pad, gt_pad, conv_w, alog_row, dtb_row, gn_row, s0_bd, n_valid, nb):
    n_seq = qkv_pad.shape[0]
    blk = lambda i: (i, 0, 0)
    blk4 = lambda i: (i, 0, 0, 0)
    const2 = lambda i: (0, 0)
    n_pack = N_HEADS // HEADS_PER_PACK
    return pl.pallas_call(
        functools.partial(_gdn_kernel, nb=nb, rows=CHUNK, carry=False, n_valid=n_valid, precise=True),
        grid=(n_seq // nb,),
        in_specs=[
            pl.BlockSpec((nb, CHUNK, QKV_W), blk),
            pl.BlockSpec((nb, 8, QKV_W), blk),
            pl.BlockSpec((nb, CHUNK, GROUP_W), blk),
            pl.BlockSpec((nb, CHUNK, LANES), blk),
            pl.BlockSpec((CONV_TAPS, QKV_W), const2),
            pl.BlockSpec((1, LANES), const2),
            pl.BlockSpec((1, LANES), const2),
            pl.BlockSpec((1, GROUP_W), const2),
            pl.BlockSpec((nb, n_pack, PACK_W, PACK_W), blk4),
        ],
        out_specs=[
            pl.BlockSpec((nb, CHUNK, GROUP_W), blk),
            pl.BlockSpec((nb, n_pack, PACK_W, PACK_W), blk4),
        ],
        out_shape=[
            jax.ShapeDtypeStruct((n_seq, CHUNK, GROUP_W), F32),
            jax.ShapeDtypeStruct((n_seq, n_pack, PACK_W, PACK_W), F32),
        ],
        scratch_shapes=_gdn_scratch(nb, CHUNK),
        compiler_params=pltpu.CompilerParams(dimension_semantics=("arbitrary",), vmem_limit_bytes=VMEM_LIMIT),
        name="gdn_sample",
    )(qkv_pad, prev_pad, z_pad, gt_pad, conv_w, alog_row, dtb_row, gn_row, s0_bd)


def _state_to_block_diag(s):
    n = s.shape[0]
    n_pack = N_HEADS // HEADS_PER_PACK
    s5 = s.reshape(n, n_pack, HEADS_PER_PACK, HEAD_DIM, HEAD_DIM).astype(F32)
    eye = jnp.eye(HEADS_PER_PACK, dtype=F32)
    return jnp.einsum("bghde,hk->bghdke", s5, eye).reshape(n, n_pack, PACK_W, PACK_W)


def _state_from_block_diag(s_bd):
    n = s_bd.shape[0]
    n_pack = N_HEADS // HEADS_PER_PACK
    s6 = s_bd.reshape(n, n_pack, HEADS_PER_PACK, HEAD_DIM, HEADS_PER_PACK, HEAD_DIM)
    diag = jnp.stack([s6[:, :, h, :, h, :] for h in range(HEADS_PER_PACK)], axis=2)
    return diag.reshape(n, N_HEADS, HEAD_DIM, HEAD_DIM)


def _swa_kernel(q_ref, kc_ref, kp_ref, vc_ref, vp_ref, o_ref, l_ref, *, tq):
    first_block = pl.program_id(2) == 0
    n_sub = tq // BAND
    lane = lax.broadcasted_iota(I32, (1, LANES), 1)
    low = lane < HEAD_DIM
    qi = lax.broadcasted_iota(I32, (BAND, 2 * BAND), 0)
    kj = lax.broadcasted_iota(I32, (BAND, 2 * BAND), 1)
    dist = qi + BAND - kj
    band = (dist >= 0) & (dist <= BAND)
    band_first = band & ((kj >= BAND) | jnp.logical_not(first_block))
    lane_l = lax.broadcasted_iota(I32, (BAND, LANES), 1)
    lse_acc = [jnp.zeros((BAND, LANES), F32) for _ in range(n_sub)]
    for p in range(GROUP_W // LANES):
        ls = slice(p * LANES, (p + 1) * LANES)
        q_pair = (q_ref[:, ls].astype(F32) * (HEAD_DIM ** -0.5)).astype(BF16)
        k_all = jnp.concatenate([kp_ref[:, ls], kc_ref[:, ls]], axis=0)
        v_all = jnp.concatenate([vp_ref[:, ls], vc_ref[:, ls]], axis=0)
        o_sub = [jnp.zeros((BAND, LANES), F32) for _ in range(n_sub)]
        for hh in range(2):
            hmask = low if hh == 0 else jnp.logical_not(low)
            k_h = jnp.where(hmask, k_all, jnp.zeros_like(k_all))
            v_h = jnp.where(hmask, v_all, jnp.zeros_like(v_all))
            head = 2 * p + hh
            for s in range(n_sub):
                sc = _dot_nt(q_pair[s * BAND:(s + 1) * BAND], k_h[s * BAND:s * BAND + 2 * BAND])
                valid = band_first if s == 0 else band
                sc = jnp.where(valid, sc, NEG_BIG)
                m = jnp.max(sc, axis=1, keepdims=True)
                pm = jnp.exp(sc - m)
                den = jnp.sum(pm, axis=1, keepdims=True)
                num = _dot(pm.astype(BF16), v_h[s * BAND:s * BAND + 2 * BAND])
                o_sub[s] = o_sub[s] + num / den
                lse_acc[s] = lse_acc[s] + jnp.where(lane_l == head, m + jnp.log(den), 0.0)
        for s in range(n_sub):
            o_ref[s * BAND:(s + 1) * BAND, ls] = o_sub[s].astype(o_ref.dtype)
    for s in range(n_sub):
        l_ref[s * BAND:(s + 1) * BAND, :] = lse_acc[s]


def _swa_call(qv, kv, vv, dil, batch, seq):
    sub_len = seq // dil
    tq = min(512, sub_len)
    nq = sub_len // tq
    rows = batch * seq // dil
    cur = lambda b, r, i: (b * nq + i, r)
    prv = lambda b, r, i: (jnp.maximum((b * nq + i) * (tq // BAND) - 1, 0), r)
    return pl.pallas_call(
        functools.partial(_swa_kernel, tq=tq),
        grid=(batch, dil, nq),
        in_specs=[
            pl.BlockSpec((tq, GROUP_W), cur),
            pl.BlockSpec((tq, GROUP_W), cur),
            pl.BlockSpec((BAND, GROUP_W), prv),
            pl.BlockSpec((tq, GROUP_W), cur),
            pl.BlockSpec((BAND, GROUP_W), prv),
        ],
        out_specs=[pl.BlockSpec((tq, GROUP_W), cur), pl.BlockSpec((tq, LANES), cur)],
        out_shape=[jax.ShapeDtypeStruct((rows, dil * GROUP_W), BF16),
                   jax.ShapeDtypeStruct((rows, dil * LANES), F32)],
        compiler_params=pltpu.CompilerParams(dimension_semantics=("arbitrary",) * 3, vmem_limit_bytes=VMEM_LIMIT),
        name=f"swa_d{dil}",
    )(qv, kv, kv, vv, vv)


T_PAD = 8


def _swa_sample_kernel(*refs, t_len, n_buf, n_alias):
    q_ref, kn_ref, vn_ref, ktail_ref, vtail_ref, kc_ref, vc_ref = refs[:7]
    nk_ref, nv_ref, o_ref, l_ref = refs[7 + n_alias:]
    t_row = lax.broadcasted_iota(I32, (T_PAD, n_buf), 0)
    delta_c = n_buf + t_row - lax.broadcasted_iota(I32, (T_PAD, n_buf), 1)
    t_row_n = lax.broadcasted_iota(I32, (T_PAD, LANES), 0)
    col_n = lax.broadcasted_iota(I32, (T_PAD, LANES), 1)
    delta_n = t_row_n - col_n
    lane = lax.broadcasted_iota(I32, (HEAD_DIM, LANES), 1)
    masks = []
    for window, dil in DILATIONS:
        masks.append(((delta_c >= 0) & ((delta_c & (dil - 1)) == 0) & (delta_c <= window),
                      (col_n < t_len) & (delta_n >= 0) & ((delta_n & (dil - 1)) == 0) & (delta_n <= window)))

    def x3(a_parts, b_parts, nt):
        (ah, al), (bh, bl) = a_parts, b_parts
        f = _dot_nt if nt else _dot
        r = f(jnp.concatenate([ah, al], axis=0), bh)
        return r[:T_PAD] + r[T_PAD:] + f(ah, bl)

    for h in range(N_HEADS):
        kt = kc_ref[h]
        vt = vc_ref[h]
        for src, tail_ref, dst in ((kt, ktail_ref, nk_ref), (vt, vtail_ref, nv_ref)):
            rolled = pltpu.roll(src, n_buf - t_len, 1)
            dst[h, :, 0:n_buf - LANES] = rolled[:, 0:n_buf - LANES]
            dst[h, :, n_buf - LANES:n_buf] = jnp.where(lane >= LANES - t_len, tail_ref[h], rolled[:, n_buf - LANES:])
        q_p = _split2(q_ref[h] * (HEAD_DIM ** -0.5))
        s_c = x3(q_p, _split2(kt), False)
        s_n = x3(q_p, _split2(kn_ref[h]), True)
        vt_p = _split2(vt)
        vn_p = _split2(vn_ref[h])
        for bi, (valid_c, valid_n) in enumerate(masks):
            c0 = max(0, (n_buf - DILATIONS[bi][0] - t_len) // LANES * LANES)
            sc = jnp.where(valid_c[:, c0:], s_c[:, c0:], NEG_BIG)
            sn = jnp.where(valid_n, s_n, NEG_BIG)
            m = jnp.maximum(jnp.max(sc, axis=1, keepdims=True), jnp.max(sn, axis=1, keepdims=True))
            pc = jnp.exp(sc - m)
            pn = jnp.exp(sn - m)
            den = jnp.sum(pc, axis=1, keepdims=True) + jnp.sum(pn, axis=1, keepdims=True)
            num = x3(_split2(pc), (vt_p[0][:, c0:], vt_p[1][:, c0:]), True) + x3(_split2(pn), vn_p, False)
            o_ref[h, bi * T_PAD:(bi + 1) * T_PAD, :] = num / den
            l_ref[h, bi * T_PAD:(bi + 1) * T_PAD, :] = jnp.broadcast_to(m + jnp.log(den), (T_PAD, LANES))


def _swa_sample_call(q, k_new, v_new, cache_kt, cache_vt, layer, new_kt_all, new_vt_all):
    nbatch, t_len = q.shape[:2]
    n_buf = cache_kt.shape[-1]
    nd = len(DILATIONS)
    assert t_len <= T_PAD and n_buf % LANES == 0 and n_buf >= max(w for w, _ in DILATIONS)

    def head_major(x, rows):
        return jnp.pad(x.transpose(0, 2, 1, 3), ((0, 0), (0, 0), (0, rows - t_len), (0, 0)))

    def tail(x):
        return jnp.pad(x.transpose(0, 2, 3, 1), ((0, 0), (0, 0), (0, 0), (LANES - t_len, 0)))

    def spec(rows, width):
        return pl.BlockSpec((None, N_HEADS, rows, width), lambda b: (b, 0, 0, 0))

    cache = pl.BlockSpec((None, None, N_HEADS, HEAD_DIM, n_buf), lambda b: (layer, b, 0, 0, 0))
    in_specs = [spec(T_PAD, HEAD_DIM), spec(LANES, HEAD_DIM), spec(LANES, HEAD_DIM), spec(HEAD_DIM, LANES),
                spec(HEAD_DIM, LANES), cache, cache]
    args = [head_major(q, T_PAD), head_major(k_new, LANES), head_major(v_new, LANES), tail(k_new), tail(v_new),
            cache_kt, cache_vt]
    aliases = {}
    if new_kt_all is not None:
        in_specs += [pl.BlockSpec(memory_space=pl.ANY)] * 2
        args += [new_kt_all, new_vt_all]
        aliases = {7: 0, 8: 1}
    return pl.pallas_call(
        functools.partial(_swa_sample_kernel, t_len=t_len, n_buf=n_buf, n_alias=len(aliases)),
        grid=(nbatch,),
        in_specs=in_specs,
        out_specs=[cache, cache, spec(nd * T_PAD, HEAD_DIM), spec(nd * T_PAD, LANES)],
        out_shape=[jax.ShapeDtypeStruct(cache_kt.shape, F32), jax.ShapeDtypeStruct(cache_vt.shape, F32),
                   jax.ShapeDtypeStruct((nbatch, N_HEADS, nd * T_PAD, HEAD_DIM), F32),
                   jax.ShapeDtypeStruct((nbatch, N_HEADS, nd * T_PAD, LANES), F32)],
        input_output_aliases=aliases,
        compiler_params=pltpu.CompilerParams(dimension_semantics=("arbitrary",), vmem_limit_bytes=VMEM_LIMIT),
        name="swa_sample",
    )(*args)


def _outproj_kernel(*refs, route, n_dtype, precise, dils):
    oa_ref = refs[0]
    o_refs = refs[1:4]
    l_refs = refs[4:7]
    x_ref, w_ref, g_ref = refs[7:10]
    nxt = 10
    if route:
        r_ref = refs[nxt]
        nxt += 1
    n_perm = sum(1 for d in dils if d > 1)
    perm_refs = refs[nxt:nxt + n_perm]
    nxt += n_perm
    if route:
        x1_ref, n_ref, ri_ref, rw_ref = refs[nxt:]
    else:
        x1_ref, n_ref = refs[nxt:]

    def token_order(ref, d, pi, width, exact_f32):
        if d == 1:
            return ref[...]
        stacked = jnp.concatenate([ref[:, r * width:(r + 1) * width] for r in range(d)], axis=0)
        pt = perm_refs[pi][...]
        if exact_f32:
            return sum(_dot(pt, part) for part in _split3(stacked))
        return _dot(pt, stacked)

    o_vals, lses, pi = [], [], 0
    for d, o_ref, l_ref in zip(dils, o_refs, l_refs):
        o_vals.append(token_order(o_ref, d, pi, GROUP_W, False))
        lses.append(token_order(l_ref, d, pi, LANES, True))
        pi += 1 if d > 1 else 0
    mx = jnp.maximum(jnp.maximum(lses[0], lses[1]), lses[2])
    es = [jnp.exp(l - mx) for l in lses]
    inv = 1.0 / (es[0] + es[1] + es[2])
    er = lax.broadcasted_iota(I32, (LANES, GROUP_W), 0)
    ec = lax.broadcasted_iota(I32, (LANES, GROUP_W), 1)
    expand = jnp.where(er == ec // HEAD_DIM, 1.0, 0.0).astype(BF16)
    ob = None
    for e, o_val in zip(es, o_vals):
        hi, lo = _split2(e * inv)
        term = (_dot(hi, expand) + _dot(lo, expand)) * o_val
        ob = term if ob is None else ob + term
    mixed = _mm(oa_ref[...], w_ref[0:GROUP_W, :], precise) + _mm(ob, w_ref[GROUP_W:, :], precise)
    x1 = x_ref[...] + mixed
    x1_ref[...] = x1
    n = _rmsnorm(x1, g_ref[...])
    n_ref[...] = n.astype(n_dtype)
    if route:
        logits = _mm(n, r_ref[...], True) if precise else _dot_x3(n, r_ref[...])
        lane = lax.broadcasted_iota(I32, logits.shape, 1)
        logits = jnp.where(lane < N_EXPERTS, logits, NEG_BIG)
        m1 = jnp.max(logits, axis=1, keepdims=True)
        i1 = jnp.min(jnp.where(logits == m1, lane, LANES), axis=1, keepdims=True)
        rest = jnp.where(lane == i1, NEG_BIG, logits)
        m2 = jnp.max(rest, axis=1, keepdims=True)
        i2 = jnp.min(jnp.where(rest == m2, lane, LANES), axis=1, keepdims=True)
        e2 = jnp.exp(m2 - m1)
        w1 = 1.0 / (1.0 + e2)
        w2 = e2 * w1
        ri_ref[...] = jnp.where(lane == 0, i1, jnp.where(lane == 1, i2, 0))
        rw_ref[...] = jnp.where(lane == 0, w1, jnp.where(lane == 1, w2, 0.0))


def _outproj_call(oa, o_br, l_br, x, w_out, g, router_pad, tm, n_dtype, precise=False, dils=(1, 1, 1)):
    m = x.shape[0]
    route = router_pad is not None
    row = lambda i: (i, 0)
    const = lambda i: (0, 0)
    in_specs = ([pl.BlockSpec((tm, GROUP_W), row)]
                + [pl.BlockSpec((tm // d, d * GROUP_W), row) for d in dils]
                + [pl.BlockSpec((tm // d, d * LANES), row) for d in dils]
                + [pl.BlockSpec((tm, D_MODEL), row), pl.BlockSpec((D_MODEL, D_MODEL), const),
                   pl.BlockSpec((1, D_MODEL), const)])
    args = [oa, *o_br, *l_br, x, w_out, g]
    if route:
        in_specs.append(pl.BlockSpec((D_MODEL, LANES), const))
        args.append(router_pad)
    for d in dils:
        if d > 1:
            in_specs.append(pl.BlockSpec((tm, tm), const))
            args.append(jnp.asarray(_residue_perm(tm, d).T, BF16))
    out_specs = [pl.BlockSpec((tm, D_MODEL), row)] * 2
    out_shape = [jax.ShapeDtypeStruct((m, D_MODEL), F32), jax.ShapeDtypeStruct((m, D_MODEL), n_dtype)]
    if route:
        out_specs += [pl.BlockSpec((tm, LANES), row)] * 2
        out_shape += [jax.ShapeDtypeStruct((m, LANES), I32), jax.ShapeDtypeStruct((m, LANES), F32)]
    return pl.pallas_call(
        functools.partial(_outproj_kernel, route=route, n_dtype=n_dtype, precise=precise, dils=tuple(dils)),
        grid=(m // tm,),
        in_specs=in_specs,
        out_specs=out_specs,
        out_shape=out_shape,
        compiler_params=pltpu.CompilerParams(dimension_semantics=("arbitrary",), vmem_limit_bytes=VMEM_LIMIT),
        name="outproj",
    )(*args)


def _ffn_kernel(n_ref, x1_ref, wg_ref, wu_ref, wd_ref, o_ref, *, f_chunk):
    n = n_ref[...]
    d_ff = wg_ref.shape[1]
    acc = x1_ref[...]
    for f0 in range(0, d_ff, f_chunk):
        h = _silu(_mm(n, wg_ref[:, f0:f0 + f_chunk])) * _mm(n, wu_ref[:, f0:f0 + f_chunk])
        acc = acc + _mm(h, wd_ref[f0:f0 + f_chunk, :])
    o_ref[...] = acc


def _ffn_call(n, x1, wg, wu, wd, tm):
    m = n.shape[0]
    d_ff = wg.shape[1]
    f_chunk = d_ff // 2 if (d_ff // 2) % LANES == 0 else d_ff
    row = lambda i: (i, 0)
    const = lambda i: (0, 0)
    return pl.pallas_call(
        functools.partial(_ffn_kernel, f_chunk=f_chunk),
        grid=(m // tm,),
        in_specs=[pl.BlockSpec((tm, D_MODEL), row), pl.BlockSpec((tm, D_MODEL), row),
                  pl.BlockSpec((D_MODEL, d_ff), const), pl.BlockSpec((D_MODEL, d_ff), const),
                  pl.BlockSpec((d_ff, D_MODEL), const)],
        out_specs=pl.BlockSpec((tm, D_MODEL), row),
        out_shape=jax.ShapeDtypeStruct((m, D_MODEL), F32),
        compiler_params=pltpu.CompilerParams(dimension_semantics=("arbitrary",), vmem_limit_bytes=VMEM_LIMIT),
        name="ffn_dense",
    )(n, x1, wg, wu, wd)


def _ffn_stream_kernel(n_ref, x1_ref, wg_ref, wu_ref, wd_ref, o_ref, *, precise):
    @pl.when(pl.program_id(0) == 0)
    def _():
        o_ref[...] = x1_ref[...]

    n = n_ref[...]
    h = _silu(_mm(n, wg_ref[...], precise)) * _mm(n, wu_ref[...], precise)
    o_ref[...] += _mm(h, wd_ref[...], precise)


def _ffn_stream_call(n, x1, wg, wu, wd, tf, precise):
    m = n.shape[0]
    d_ff = wg.shape[1]
    const = lambda j: (0, 0)
    return pl.pallas_call(
        functools.partial(_ffn_stream_kernel, precise=precise),
        grid=(d_ff // tf,),
        in_specs=[pl.BlockSpec((m, D_MODEL), const), pl.BlockSpec((m, D_MODEL), const),
                  pl.BlockSpec((D_MODEL, tf), lambda j: (0, j)), pl.BlockSpec((D_MODEL, tf), lambda j: (0, j)),
                  pl.BlockSpec((tf, D_MODEL), lambda j: (j, 0))],
        out_specs=pl.BlockSpec((m, D_MODEL), const),
        out_shape=jax.ShapeDtypeStruct((m, D_MODEL), F32),
        compiler_params=pltpu.CompilerParams(dimension_semantics=("arbitrary",), vmem_limit_bytes=VMEM_LIMIT),
        name="ffn_stream",
    )(n, x1, wg, wu, wd)


def _moe_kernel(te_ref, tv_ref, src_ref, nxt_ref, n_hbm, wg_ref, wu_ref, wd_ref, y_ref, x_s, xb_s, acc_s, sems, *,
                tme, nj):
    i = pl.program_id(0)
    j = pl.program_id(1)
    n_tiles = pl.num_programs(0)
    slot = i % 2
    share = tme // nj

    def tile_wait(s):
        pltpu.make_async_copy(n_hbm.at[pl.ds(0, tme), :], x_s.at[s], sems.at[s]).wait()

    @pl.when(tv_ref[i] > 0)
    def _():
        @pl.when((i == 0) & (j == 0))
        def _():
            def issue(r, c):
                pltpu.make_async_copy(n_hbm.at[pl.ds(src_ref[0, 0, r], 1), :], x_s.at[0, pl.ds(r, 1), :],
                                      sems.at[0]).start()
                return c
            lax.fori_loop(0, tme, issue, 0, unroll=8)

        @pl.when(j == 0)
        def _():
            tile_wait(slot)
            xb_s[...] = x_s[slot].astype(BF16)
            acc_s[...] = jnp.zeros_like(acc_s)

        x = xb_s[...]
        h = _silu(_mm(x, wg_ref[0])) * _mm(x, wu_ref[0])
        acc_s[...] += _mm(h, wd_ref[0])
        for k in range(share):
            r = j * share + k
            pltpu.make_async_copy(n_hbm.at[pl.ds(nxt_ref[0, 0, r], 1), :], x_s.at[1 - slot, pl.ds(r, 1), :],
                                  sems.at[1 - slot]).start()

        @pl.when(j == nj - 1)
        def _():
            y_ref[...] = acc_s[...]

            @pl.when(tv_ref[jnp.minimum(i + 1, n_tiles - 1)] * (i + 1 < n_tiles).astype(I32) == 0)
            def _():
                tile_wait(1 - slot)

    @pl.when((tv_ref[i] == 0) & (j == nj - 1))
    def _():
        y_ref[...] = jnp.zeros_like(y_ref)


def _moe_call(n_all, tile_expert, tile_valid, src_tok3, wg, wu, wd, tme, tf):
    n_tiles = tile_expert.shape[0]
    d_ff = wg.shape[2]
    nj = d_ff // tf
    assert tme % nj == 0 and tme % 8 == 0

    def jsel(i, j, te, tv):
        return jnp.where(tv[i] > 0, j, nj - 1)

    grid_spec = pltpu.PrefetchScalarGridSpec(
        num_scalar_prefetch=2,
        grid=(n_tiles, nj),
        in_specs=[
            pl.BlockSpec((1, 1, tme), lambda i, j, te, tv: (i, 0, 0), memory_space=pltpu.SMEM),
            pl.BlockSpec((1, 1, tme), lambda i, j, te, tv: (jnp.minimum(i + 1, n_tiles - 1), 0, 0),
                         memory_space=pltpu.SMEM),
            pl.BlockSpec(memory_space=pl.ANY),
            pl.BlockSpec((1, D_MODEL, tf), lambda i, j, te, tv: (te[i], 0, jsel(i, j, te, tv))),
            pl.BlockSpec((1, D_MODEL, tf), lambda i, j, te, tv: (te[i], 0, jsel(i, j, te, tv))),
            pl.BlockSpec((1, tf, D_MODEL), lambda i, j, te, tv: (te[i], jsel(i, j, te, tv), 0)),
        ],
        out_specs=pl.BlockSpec((tme, D_MODEL), lambda i, j, te, tv: (i, 0)),
        scratch_shapes=[pltpu.VMEM((2, tme, D_MODEL), F32), pltpu.VMEM((tme, D_MODEL), BF16),
                        pltpu.VMEM((tme, D_MODEL), F32), pltpu.SemaphoreType.DMA((2,))],
    )
    return pl.pallas_call(
        functools.partial(_moe_kernel, tme=tme, nj=nj),
        grid_spec=grid_spec,
        out_shape=jax.ShapeDtypeStruct((n_tiles * tme, D_MODEL), F32),
        compiler_params=pltpu.CompilerParams(dimension_semantics=("arbitrary", "arbitrary"), vmem_limit_bytes=VMEM_LIMIT),
        name="moe_experts",
    )(tile_expert, tile_valid, src_tok3, src_tok3, n_all, wg, wu, wd)


def _route_plan(route_idx, tme, n_tiles):
    n_tok = route_idx.shape[0]
    e_flat = route_idx.reshape(-1)
    onehot = (e_flat[:, None] == jnp.arange(N_EXPERTS, dtype=I32)[None, :]).astype(I32)
    csum = jnp.cumsum(onehot, axis=0)
    rank = jnp.sum(csum * onehot, axis=1) - 1
    cnt = csum[-1]
    tiles_e = (cnt + tme - 1) // tme
    tile_end = jnp.cumsum(tiles_e)
    row_off = (tile_end - tiles_e) * tme
    dst = jnp.sum(onehot * row_off[None, :], axis=1) + rank
    tile_id = jnp.arange(n_tiles, dtype=I32)
    tile_expert = jnp.minimum(jnp.sum((tile_id[:, None] >= tile_end[None, :]).astype(I32), axis=1), N_EXPERTS - 1)
    tile_valid = (tile_id < tile_end[-1]).astype(I32)
    tok_of = jnp.arange(2 * n_tok, dtype=I32) // 2
    src_tok = jnp.zeros((n_tiles * tme,), I32).at[dst].set(tok_of)
    return dst.reshape(n_tok, 2), src_tok.reshape(n_tiles, 1, tme), tile_expert.astype(I32), tile_valid


def _moe_combine_kernel(pos_ref, y_hbm, x1_ref, rw_ref, g_ref, o_ref, y_s, sem, *, tm, final_norm):
    def issue(half, c):
        for prio in range(2):
            r = 2 * half + prio
            for slot in range(2):
                pos = pos_ref[0, 0, 2 * r + slot]
                pltpu.make_async_copy(y_hbm.at[pl.ds(pos, 1), :], y_s.at[slot, pl.ds(r, 1), :], sem).start(
                    priority=prio)
        return c
    lax.fori_loop(0, tm // 2, issue, 0, unroll=4)
    for slot in range(2):
        pltpu.make_async_copy(y_hbm.at[pl.ds(0, tm), :], y_s.at[slot], sem).wait()
    rw = rw_ref[...]
    out = x1_ref[...] + rw[:, 0:1] * y_s[0] + rw[:, 1:2] * y_s[1]
    if final_norm:
        out = _rmsnorm(out, g_ref[...])
    o_ref[...] = out


def _moe_combine_call(pos3, y_sorted, x1, rw, g_final, tm, final_norm):
    m = x1.shape[0]
    row = lambda i: (i, 0)
    return pl.pallas_call(
        functools.partial(_moe_combine_kernel, tm=tm, final_norm=final_norm),
        grid=(m // tm,),
        in_specs=[
            pl.BlockSpec((1, 1, 2 * tm), lambda i: (i, 0, 0), memory_space=pltpu.SMEM),
            pl.BlockSpec(memory_space=pl.ANY),
            pl.BlockSpec((tm, D_MODEL), row),
            pl.BlockSpec((tm, LANES), row),
            pl.BlockSpec((1, D_MODEL), lambda i: (0, 0)),
        ],
        out_specs=pl.BlockSpec((tm, D_MODEL), row),
        out_shape=jax.ShapeDtypeStruct((m, D_MODEL), F32),
        scratch_shapes=[pltpu.VMEM((2, tm, D_MODEL), F32), pltpu.SemaphoreType.DMA],
        compiler_params=pltpu.CompilerParams(dimension_semantics=("arbitrary",), vmem_limit_bytes=VMEM_LIMIT),
        name="moe_combine",
    )(pos3, y_sorted, x1, rw, g_final)


def _final_norm_kernel(x_ref, g_ref, o_ref):
    o_ref[...] = _rmsnorm(x_ref[...], g_ref[...])


def _final_norm_call(x, g, tm):
    m = x.shape[0]
    row = lambda i: (i, 0)
    return pl.pallas_call(
        _final_norm_kernel,
        grid=(m // tm,),
        in_specs=[pl.BlockSpec((tm, D_MODEL), row), pl.BlockSpec((1, D_MODEL), lambda i: (0, 0))],
        out_specs=pl.BlockSpec((tm, D_MODEL), row),
        out_shape=jax.ShapeDtypeStruct((m, D_MODEL), F32),
        compiler_params=pltpu.CompilerParams(dimension_semantics=("arbitrary",)),
        name="final_norm",
    )(x, g)


def _rope_tables(pos):
    half = HEAD_DIM // 2
    inv_freq = np.power(np.float64(ROPE_THETA), -np.arange(half, dtype=np.float64) * (2.0 / HEAD_DIM))
    ang = np.asarray(pos, np.float64)[:, None] * inv_freq[None, :]
    c, s = np.cos(ang), np.sin(ang)
    cos_t = np.concatenate([c, c, c, c], axis=1).astype(np.float32)
    sin_t = np.concatenate([-s, s, -s, s], axis=1).astype(np.float32)
    return jnp.asarray(cos_t), jnp.asarray(sin_t)


def _pack_w_in(w):
    gates0 = QKV_W + GROUP_W
    pad = jnp.zeros((w.shape[0], LANES - 2 * N_HEADS), w.dtype)
    return jnp.concatenate([w[:, :gates0], w[:, gates0 + 2 * N_HEADS:], w[:, gates0:gates0 + 2 * N_HEADS], pad],
                           axis=1)


def _lane_row(vals, offset):
    return jnp.zeros((1, LANES), F32).at[0, offset:offset + vals.shape[0]].set(vals.astype(F32))


def _pad_rows(x, rows, front=0):
    return jnp.pad(x, ((0, 0), (front, rows - front - x.shape[1]), (0, 0)))


def kernel(x_prompt, x_sample, state_conv, state_delta, cache_win_k, cache_win_v, norm_mix, w_in, conv_w, a_log,
           dt_bias, gdn_norm, w_out, norm_ffn, ffn_gate, ffn_up, ffn_down, router, moe_gate, moe_up, moe_down,
           final_norm):
    bp, seq, _ = x_prompt.shape
    bs, t_len, _ = x_sample.shape
    depth = w_in.shape[0]
    n_p, n_s = bp * seq, bs * t_len
    n_all = n_p + n_s
    tm_p, tm_s = 512, n_s
    gdn_rows = 256
    n_buf = cache_win_k.shape[2]
    n_keep = min(WINDOW_MAX, seq)

    cos_p, sin_p = _rope_tables(np.arange(seq))
    cos_s, sin_s = _rope_tables(PAST_LEN + (np.arange(n_s) % t_len))
    xp = x_prompt.reshape(n_p, D_MODEL)
    xs = x_sample.reshape(n_s, D_MODEL)
    g_final = final_norm.reshape(1, D_MODEL)

    conv_p, delta_p, wk_p, wv_p = [], [], [], []
    conv_s, delta_s = [], []
    new_k_all = new_v_all = None
    cache_kt = cache_win_k.astype(F32).transpose(0, 1, 3, 4, 2)
    cache_vt = cache_win_v.astype(F32).transpose(0, 1, 3, 4, 2)
    for l in range(depth):
        w_packed_f = _pack_w_in(w_in[l].astype(F32))
        w_packed = w_packed_f.astype(BF16)
        w_out_f = w_out[l].astype(F32)
        g_mix = norm_mix[l].reshape(1, D_MODEL)
        alog_row = _lane_row(a_log[l], N_HEADS)
        dtb_row = _lane_row(dt_bias[l], N_HEADS)
        gn_row = jnp.tile(gdn_norm[l].astype(F32), N_HEADS).reshape(1, GROUP_W)
        w_out_b = w_out[l].astype(BF16)
        g_ffn = norm_ffn[l].reshape(1, D_MODEL)
        is_moe = l % 2 == 1
        idx = l // 2

        dils = tuple(d for _, d in DILATIONS)
        qkv, z, gt, k, v, *qkv_d = _inproj_call(xp, g_mix, w_packed, cos_p, sin_p, tm_p, dils=dils)
        oa, s_bd = _gdn_prompt_call(qkv, z, gt, conv_w[l], alog_row, dtb_row, gn_row, bp, seq, gdn_rows)
        branches = [_swa_call(*qkv_d[3 * i:3 * i + 3], dil, bp, seq) for i, dil in enumerate(dils)]
        conv_p.append(qkv.reshape(bp, seq, QKV_W)[:, seq - (CONV_TAPS - 1):])
        delta_p.append(_state_from_block_diag(s_bd))
        wk_p.append(k.reshape(bp, seq, GROUP_W)[:, seq - n_keep:].reshape(bp, n_keep, N_HEADS, HEAD_DIM))
        wv_p.append(v.reshape(bp, seq, GROUP_W)[:, seq - n_keep:].reshape(bp, n_keep, N_HEADS, HEAD_DIM))

        qkv_s, z_s, gt_s, q_s, k_s, v_s = _inproj_call(xs, g_mix, w_packed_f, cos_s, sin_s, tm_s, precise=True)
        prev_pad = _pad_rows(state_conv[l].astype(F32), 8, front=8 - (CONV_TAPS - 1))
        oa_s_pad, s_bd_s = _gdn_sample_call(
            _pad_rows(qkv_s.reshape(bs, t_len, QKV_W), CHUNK), prev_pad,
            _pad_rows(z_s.reshape(bs, t_len, GROUP_W), CHUNK), _pad_rows(gt_s.reshape(bs, t_len, LANES), CHUNK),
            conv_w[l], alog_row, dtb_row, gn_row, _state_to_block_diag(state_delta[l]), t_len, min(8, bs))
        oa_s = oa_s_pad[:, :t_len].reshape(n_s, GROUP_W)
        heads = (bs, t_len, N_HEADS, HEAD_DIM)
        new_k_all, new_v_all, o_s3, l_s3 = _swa_sample_call(
            q_s.reshape(heads), k_s.reshape(heads), v_s.reshape(heads), cache_kt, cache_vt, l, new_k_all, new_v_all)
        conv_all = jnp.concatenate([state_conv[l].astype(F32), qkv_s.reshape(bs, t_len, QKV_W)], axis=1)
        conv_s.append(conv_all[:, conv_all.shape[1] - (CONV_TAPS - 1):])
        delta_s.append(_state_from_block_diag(s_bd_s))
        nd = len(DILATIONS)
        o_br_s = [o_s3[:, :, i * T_PAD:i * T_PAD + t_len].transpose(0, 2, 1, 3).reshape(n_s, GROUP_W)
                  for i in range(nd)]
        l_br_s = [jnp.pad(l_s3[:, :, i * T_PAD:i * T_PAD + t_len, 0].transpose(0, 2, 1).reshape(n_s, N_HEADS),
                          ((0, 0), (0, LANES - N_HEADS))) for i in range(nd)]

        o_br_p = [b[0] for b in branches]
        l_br_p = [b[1] for b in branches]
        if not is_moe:
            x1p, np_ = _outproj_call(oa.reshape(n_p, GROUP_W), o_br_p, l_br_p, xp, w_out_b, g_ffn, None, tm_p, BF16,
                                     dils=dils)
            x1s, ns_ = _outproj_call(oa_s, o_br_s, l_br_s, xs, w_out_f, g_ffn, None, tm_s, F32, precise=True)
            wg, wu, wd = ffn_gate[idx].astype(BF16), ffn_up[idx].astype(BF16), ffn_down[idx].astype(BF16)
            xp = _ffn_call(np_, x1p, wg, wu, wd, tm_p)
            xs = _ffn_stream_call(ns_, x1s, ffn_gate[idx].astype(F32), ffn_up[idx].astype(F32),
                                  ffn_down[idx].astype(F32), 256, precise=True)
            if l == depth - 1:
                xp = _final_norm_call(xp, g_final, tm_p)
                xs = _final_norm_call(xs, g_final, tm_s)
        else:
            router_pad = jnp.pad(router[idx].astype(F32), ((0, 0), (0, LANES - N_EXPERTS)))
            x1p, n_tok_p, ri_p, rw_p = _outproj_call(
                oa.reshape(n_p, GROUP_W), o_br_p, l_br_p, xp, w_out_b, g_ffn, router_pad, tm_p, F32, dils=dils)
            x1s, n_tok_s, ri_s, rw_s = _outproj_call(oa_s, o_br_s, l_br_s, xs, w_out_f, g_ffn, router_pad, tm_s, F32,
                                                     precise=True)
            n_tok = jnp.concatenate([n_tok_p, n_tok_s], axis=0)
            tme, tf = 1120, 512
            n_tiles = (2 * n_all + N_EXPERTS * (tme - 1)) // tme
            ridx = jnp.concatenate([ri_p[:, :2], ri_s[:, :2]], axis=0)
            dst, src_tok3, tile_expert, tile_valid = _route_plan(ridx, tme, n_tiles)
            y_sorted = _moe_call(n_tok, tile_expert, tile_valid, src_tok3, moe_gate[idx], moe_up[idx], moe_down[idx],
                                 tme, tf)
            last = l == depth - 1
            tmc_p, tmc_s = 512, n_s
            xp = _moe_combine_call(dst[:n_p].reshape(n_p // tmc_p, 1, 2 * tmc_p), y_sorted, x1p, rw_p, g_final,
                                   tmc_p, last)
            xs = _moe_combine_call(dst[n_p:].reshape(n_s // tmc_s, 1, 2 * tmc_s), y_sorted, x1s, rw_s, g_final,
                                   tmc_s, last)

    y_prompt = xp.reshape(bp, seq, D_MODEL)
    y_sample = xs.reshape(bs, t_len, D_MODEL)
    return (y_prompt, y_sample, jnp.stack(conv_p), jnp.stack(delta_p), jnp.stack(wk_p), jnp.stack(wv_p),
            jnp.stack(conv_s), jnp.stack(delta_s), new_k_all.transpose(0, 1, 4, 2, 3),
            new_v_all.transpose(0, 1, 4, 2, 3))
```
